```python
import jax, jax.numpy as jnp
from jax import lax
import numpy as np

D_MODEL = 1024
BATCH = 1
SEQ = 16384
DEPTH = 1

CHUNK = 64
HEAD_DIM = 64
N_HEADS_SB = 8
N_HEADS_FOX = 8
D_SB = N_HEADS_SB * HEAD_DIM
D_FOX = N_HEADS_FOX * HEAD_DIM
D_FF = 4 * D_MODEL
Q_BLOCK = 128
N_IN = 3 * D_SB + 3 * D_FOX + N_HEADS_FOX
N_MOD = 6
EPS = 1e-6

kernel_name = "hybrid_stickbreak_forgetting_adaln_block"


def rmsnorm(x, g):
    xf = x.astype(jnp.float32)
    y = xf * lax.rsqrt(jnp.mean(xf * xf, axis=-1, keepdims=True) + EPS)
    return (y * g.astype(jnp.float32)).astype(x.dtype)


def modulate(h, shift, scale):
    return h * (1.0 + scale[:, None, :]) + shift[:, None, :]


def to_heads(t, n_heads):
    b, s, _ = t.shape
    return t.reshape(b, s, n_heads, HEAD_DIM).transpose(0, 2, 1, 3)


def to_query_blocks(t):
    b, h, s, d = t.shape
    return t.reshape(b, h, s // Q_BLOCK, Q_BLOCK, d).transpose(2, 0, 1, 3, 4)


def from_query_blocks(o):
    nblk, b, h, qb, d = o.shape
    return o.transpose(1, 0, 3, 2, 4).reshape(b, nblk * qb, h * d)


def stick_breaking_attention(q, k, v):
    s_len = q.shape[2]
    scale = HEAD_DIM ** -0.5
    kf = k.astype(jnp.float32)
    vf = v.astype(jnp.float32)
    key_pos = jnp.arange(s_len)

    def block(args):
        qb, i = args
        q_pos = i * Q_BLOCK + jnp.arange(Q_BLOCK)
        z = jnp.einsum('bhqd,bhkd->bhqk', qb.astype(jnp.float32), kf) * scale
        valid = key_pos[None, :] < q_pos[:, None]
        log_1m_beta = jnp.where(valid, jax.nn.log_sigmoid(-z), 0.0)
        suffix = lax.cumsum(log_1m_beta, axis=3, reverse=True) - log_1m_beta
        w = jnp.where(valid, jnp.exp(jax.nn.log_sigmoid(z) + suffix), 0.0)
        return jnp.einsum('bhqk,bhkd->bhqd', w, vf)

    out = lax.map(block, (to_query_blocks(q), jnp.arange(s_len // Q_BLOCK)))
    return from_query_blocks(out)


def forgetting_attention(q, k, v, log_f):
    b, h, s_len, _ = q.shape
    scale = HEAD_DIM ** -0.5
    kf = k.astype(jnp.float32)
    vf = v.astype(jnp.float32)
    F = jnp.cumsum(log_f.astype(jnp.float32), axis=-1)
    F_blocks = F.reshape(b, h, s_len // Q_BLOCK, Q_BLOCK).transpose(2, 0, 1, 3)
    key_pos = jnp.arange(s_len)

    def block(args):
        qb, Fq, i = args
        q_pos = i * Q_BLOCK + jnp.arange(Q_BLOCK)
        z = jnp.einsum('bhqd,bhkd->bhqk', qb.astype(jnp.float32), kf) * scale
        z = z + Fq[..., :, None] - F[:, :, None, :]
        valid = key_pos[None, :] <= q_pos[:, None]
        p = jax.nn.softmax(jnp.where(valid, z, -jnp.inf), axis=-1)
        return jnp.einsum('bhqk,bhkd->bhqd', p, vf)

    out = lax.map(block, (to_query_blocks(q), F_blocks, jnp.arange(s_len // Q_BLOCK)))
    return from_query_blocks(out)


def setup_inputs(seed: int = 0) -> dict:
    key = jax.random.key(seed)
    ks = jax.random.split(key, 20)
    f32 = jnp.float32
    nrm = lambda k, shape, s: (jax.random.normal(k, shape, f32) * s)
    D = D_MODEL
    return {
        "x": nrm(ks[0], (BATCH, SEQ, D), 1.0),
        "c": nrm(ks[1], (BATCH, D), 1.0),
        "w_ada": nrm(ks[2], (DEPTH, D, N_MOD * D), 0.2 * D ** -0.5),
        "b_ada": nrm(ks[3], (DEPTH, N_MOD * D), 0.02),
        "g_mix": 1.0 + nrm(ks[4], (DEPTH, D), 0.02),
        "w_in": nrm(ks[5], (DEPTH, D, N_IN), D ** -0.5),
        "b_f": 3.0 + nrm(ks[6], (DEPTH, N_HEADS_FOX), 0.5),
        "w_gate": nrm(ks[7], (DEPTH, D, 2 * D), D ** -0.5),
        "b_gate": nrm(ks[8], (DEPTH, 2 * D), 0.02),
        "w_out_sb": nrm(ks[9], (DEPTH, D_SB, D), D_SB ** -0.5),
        "w_out_fox": nrm(ks[10], (DEPTH, D_FOX, D), D_FOX ** -0.5),
        "w_o": nrm(ks[11], (DEPTH, D, D), D ** -0.5),
        "g_mlp": 1.0 + nrm(ks[12], (DEPTH, D), 0.02),
        "w_ff1": nrm(ks[13], (DEPTH, D, D_FF), D ** -0.5),
        "b_ff1": nrm(ks[14], (DEPTH, D_FF), 0.02),
        "w_ff2": nrm(ks[15], (DEPTH, D_FF, D), D_FF ** -0.5),
        "b_ff2": nrm(ks[16], (DEPTH, D), 0.02),
        "g_final": 1.0 + nrm(ks[17], (D,), 0.02),
    }


def reference(x, c, w_ada, b_ada, g_mix, w_in, b_f, w_gate, b_gate, w_out_sb, w_out_fox,
              w_o, g_mlp, w_ff1, b_ff1, w_ff2, b_ff2, g_final):
    o_qkv_sb = 3 * D_SB
    o_qkv_fox = o_qkv_sb + 3 * D_FOX
    for layer in range(DEPTH):
        mod = c @ w_ada[layer] + b_ada[layer]
        sh1, sc1, gt1, sh2, sc2, gt2 = jnp.split(mod, N_MOD, axis=-1)

        h = modulate(rmsnorm(x, g_mix[layer]), sh1, sc1)
        proj = h @ w_in[layer]
        q_sb, k_sb, v_sb = jnp.split(proj[..., :o_qkv_sb], 3, axis=-1)
        q_fx, k_fx, v_fx = jnp.split(proj[..., o_qkv_sb:o_qkv_fox], 3, axis=-1)
        log_f = jax.nn.log_sigmoid(proj[..., o_qkv_fox:].astype(jnp.float32)
                                   + b_f[layer]).transpose(0, 2, 1)

        o_sb = stick_breaking_attention(to_heads(q_sb, N_HEADS_SB), to_heads(k_sb, N_HEADS_SB),
                                        to_heads(v_sb, N_HEADS_SB)).astype(x.dtype)
        o_fx = forgetting_attention(to_heads(q_fx, N_HEADS_FOX), to_heads(k_fx, N_HEADS_FOX),
                                    to_heads(v_fx, N_HEADS_FOX), log_f).astype(x.dtype)

        gates = jax.nn.sigmoid(h @ w_gate[layer] + b_gate[layer])
        g_sb, g_fx = jnp.split(gates, 2, axis=-1)
        merged = g_sb * (o_sb @ w_out_sb[layer]) + g_fx * (o_fx @ w_out_fox[layer])
        x = x + gt1[:, None, :] * (merged @ w_o[layer])

        h = modulate(rmsnorm(x, g_mlp[layer]), sh2, sc2)
        u = jnp.square(jax.nn.relu(h @ w_ff1[layer] + b_ff1[layer]))
        x = x + gt2[:, None, :] * (u @ w_ff2[layer] + b_ff2[layer])

    return rmsnorm(x, g_final)
```

```python
import functools

import jax
import jax.numpy as jnp
from jax import lax
from jax.experimental import pallas as pl
from jax.experimental.pallas import tpu as pltpu

F32 = jnp.float32
BF16 = jnp.bfloat16

HEAD_DIM = 64
N_HEADS = 8
D_BRANCH = N_HEADS * HEAD_DIM
N_MOD = 6
EPS = 1e-6

LANES = 128
HEADS_PER_GROUP = LANES // HEAD_DIM
N_GROUPS = D_BRANCH // LANES

TS_PROJ = 512
TS_POST = 512
TQ = 256
TK = 256
FF_CHUNK = 1024

STICK_DEAD = 104.0
NEG_BIG = -1e30
VMEM_LIMIT = 56 * 1024 * 1024

_NT = (((1,), (1,)), ((), ()))


def _rmsnorm(x, g):
    return x * lax.rsqrt(jnp.mean(x * x, axis=-1, keepdims=True) + EPS) * g


def _softplus(z):
    return jnp.maximum(z, 0.0) + jnp.log(1.0 + jnp.exp(-jnp.abs(z)))


def _mod_kernel(c_ref, w_ref, b_ref, o_ref):
    o_ref[...] = jnp.dot(c_ref[...], w_ref[...], preferred_element_type=F32,
                         precision=lax.Precision.HIGHEST) + b_ref[...]


def _mod_call(c8, w_ada, b_ada):
    d, n = w_ada.shape
    return pl.pallas_call(
        _mod_kernel,
        grid=(n // d,),
        in_specs=[pl.BlockSpec((8, d), lambda j: (0, 0)),
                  pl.BlockSpec((d, d), lambda j: (0, j)),
                  pl.BlockSpec((1, d), lambda j: (0, j))],
        out_specs=pl.BlockSpec((8, d), lambda j: (0, j)),
        out_shape=jax.ShapeDtypeStruct((8, n), F32),
        name="adaln_mod",
    )(c8, w_ada, b_ada)


def _proj_kernel(x_ref, sh_ref, sc_ref, g_ref, w_ref, wf_ref, bf_ref,
                 qs_ref, ks_ref, vs_ref, qf_ref, kf_ref, vf_ref, nf_ref, carry_ref):
    ts = x_ref.shape[0]

    @pl.when(pl.program_id(0) == 0)
    def _():
        carry_ref[...] = jnp.zeros_like(carry_ref)

    h = _rmsnorm(x_ref[...], g_ref[...]) * (1.0 + sc_ref[0:1, :]) + sh_ref[0:1, :]
    hb = h.astype(BF16)
    scale = HEAD_DIM ** -0.5
    outs = (qs_ref, ks_ref, vs_ref, qf_ref, kf_ref, vf_ref)
    for j, o_ref in enumerate(outs):
        p = jnp.dot(hb, w_ref[:, j * D_BRANCH:(j + 1) * D_BRANCH], preferred_element_type=F32)
        if j % 3 == 0:
            p = p * scale
        o_ref[...] = p.astype(BF16)

    logit = jnp.dot(h, wf_ref[...], preferred_element_type=F32,
                    precision=lax.Precision.HIGHEST) + bf_ref[...]
    log_f = jnp.minimum(logit, 0.0) - jnp.log(1.0 + jnp.exp(-jnp.abs(logit)))
    log_f_t = log_f.T[0:N_HEADS, :]
    r = lax.broadcasted_iota(jnp.int32, (ts, ts), 0)
    c = lax.broadcasted_iota(jnp.int32, (ts, ts), 1)
    upper = jnp.where(r <= c, 1.0, 0.0).astype(F32)
    cum = jnp.dot(log_f_t, upper, preferred_element_type=F32,
                  precision=lax.Precision.HIGHEST) + carry_ref[:, 0:1]
    carry_ref[...] = jnp.broadcast_to(cum[:, ts - 1:ts], carry_ref.shape)
    neg = -cum
    for g in range(N_GROUPS):
        for t in range(ts // TK):
            nf_ref[g, t] = neg[g * HEADS_PER_GROUP:(g + 1) * HEADS_PER_GROUP, t * TK:(t + 1) * TK]


def _proj_call(x2, mod, g_mix, w_qkv, w_f, b_f):
    s, d = x2.shape
    ts = TS_PROJ
    n_qkv = w_qkv.shape[1]
    tok = lambda i: (i, 0)
    const = lambda i: (0, 0)
    qkv_spec = pl.BlockSpec((ts, D_BRANCH), tok)
    qkv_shape = jax.ShapeDtypeStruct((s, D_BRANCH), BF16)
    return pl.pallas_call(
        _proj_kernel,
        grid=(s // ts,),
        in_specs=[pl.BlockSpec((ts, d), tok),
                  pl.BlockSpec((8, d), lambda i: (0, 0)),
                  pl.BlockSpec((8, d), lambda i: (0, 1)),
                  pl.BlockSpec((1, d), const),
                  pl.BlockSpec((d, n_qkv), const, pipeline_mode=pl.Buffered(1)),
                  pl.BlockSpec((d, LANES), const),
                  pl.BlockSpec((1, LANES), const)],
        out_specs=[qkv_spec] * 6 + [
            pl.BlockSpec((N_GROUPS, ts // TK, HEADS_PER_GROUP, TK), lambda i: (0, i, 0, 0))],
        out_shape=[qkv_shape] * 6 + [
            jax.ShapeDtypeStruct((N_GROUPS, s // TK, HEADS_PER_GROUP, TK), F32)],
        scratch_shapes=[pltpu.VMEM((N_HEADS, LANES), F32)],
        compiler_params=pltpu.CompilerParams(dimension_semantics=("arbitrary",),
                                             vmem_limit_bytes=VMEM_LIMIT),
        name="proj",
    )(x2, mod, mod, g_mix, w_qkv, w_f, b_f)


def _head_masks():
    lane = lax.broadcasted_iota(jnp.int32, (1, LANES), 1)
    return [(lane >= h * HEAD_DIM) & (lane < (h + 1) * HEAD_DIM) for h in range(HEADS_PER_GROUP)]


def _tile_pos():
    row = lax.broadcasted_iota(jnp.int32, (TQ, TK), 0)
    col = lax.broadcasted_iota(jnp.int32, (TQ, TK), 1)
    return row, col


def _sb_kernel(q_ref, k_ref, v_ref, o_ref, acc_ref, carry_ref):
    i = pl.program_id(1)
    masks = _head_masks()
    q = q_ref[...]
    zero = jnp.zeros_like(q)
    qh = [jnp.where(m, q, zero) for m in masks]
    row, col = _tile_pos()
    strictly_before = col < row
    tri = jnp.where(row >= col, 1.0, 0.0).astype(BF16)

    acc_ref[...] = jnp.zeros_like(acc_ref)
    carry_ref[...] = jnp.zeros_like(carry_ref)

    def tile(j, diagonal):
        ks = pl.multiple_of(j * TK, TK)
        k = k_ref[pl.ds(ks, TK), :]
        v = v_ref[pl.ds(ks, TK), :]
        alive = None
        for h in range(HEADS_PER_GROUP):
            z = lax.dot_general(qh[h], k, _NT, preferred_element_type=F32)
            sp = _softplus(z)
            if diagonal:
                sp = jnp.where(strictly_before, sp, 0.0)
            sp_hi = sp.astype(BF16)
            sp_lo = (sp - sp_hi.astype(F32)).astype(BF16)
            csum = (jnp.dot(sp_hi, tri, preferred_element_type=F32)
                    + jnp.dot(sp_lo, tri, preferred_element_type=F32))
            carry = carry_ref[h]
            w = jnp.exp(z - csum - carry)
            if diagonal:
                w = jnp.where(strictly_before, w, 0.0)
            vh = jnp.where(masks[h], v, jnp.zeros_like(v))
            acc_ref[...] += jnp.dot(w.astype(BF16), vh, preferred_element_type=F32)
            new_carry = carry + csum[:, 0:1]
            carry_ref[h] = new_carry
            low = jnp.min(new_carry)
            alive = low if alive is None else jnp.minimum(alive, low)
        return alive

    low0 = tile(i, True)

    def cond(state):
        j, low = state
        return (j >= 0) & (low <= STICK_DEAD)

    def body(state):
        j, _ = state
        return j - 1, tile(j, False)

    lax.while_loop(cond, body, (i - 1, low0))
    o_ref[...] = acc_ref[...].astype(o_ref.dtype)


def _sb_call(q, k, v):
    s = q.shape[0]
    return pl.pallas_call(
        _sb_kernel,
        grid=(N_GROUPS, s // TQ),
        in_specs=[pl.BlockSpec((TQ, LANES), lambda g, i: (i, g)),
                  pl.BlockSpec((s, LANES), lambda g, i: (0, g)),
                  pl.BlockSpec((s, LANES), lambda g, i: (0, g))],
        out_specs=pl.BlockSpec((TQ, LANES), lambda g, i: (i, g)),
        out_shape=jax.ShapeDtypeStruct((s, D_BRANCH), BF16),
        scratch_shapes=[pltpu.VMEM((TQ, LANES), F32),
                        pltpu.VMEM((HEADS_PER_GROUP, TQ, 1), F32)],
        compiler_params=pltpu.CompilerParams(dimension_semantics=("arbitrary", "arbitrary"),
                                             vmem_limit_bytes=VMEM_LIMIT),
        name="stick_breaking_attn",
    )(q, k, v)


def _fox_kernel(q_ref, k_ref, v_ref, nf_ref, o_ref, acc_ref, m_ref, l_ref):
    i = pl.program_id(1)
    masks = _head_masks()
    q = q_ref[...]
    zero = jnp.zeros_like(q)
    qh = [jnp.where(m, q, zero) for m in masks]
    row, col = _tile_pos()
    not_after = col <= row

    acc_ref[...] = jnp.zeros_like(acc_ref)
    m_ref[...] = jnp.full_like(m_ref, NEG_BIG)
    l_ref[...] = jnp.zeros_like(l_ref)

    def tile(j, diagonal):
        ks = pl.multiple_of(j * TK, TK)
        k = k_ref[pl.ds(ks, TK), :]
        v = v_ref[pl.ds(ks, TK), :]
        nf = nf_ref[0, j]
        for h in range(HEADS_PER_GROUP):
            z = lax.dot_general(qh[h], k, _NT, preferred_element_type=F32)
            s = z + nf[h:h + 1, :]
            if diagonal:
                s = jnp.where(not_after, s, NEG_BIG)
            m_prev = m_ref[h]
            m_new = jnp.maximum(m_prev, jnp.max(s, axis=1, keepdims=True))
            alpha = jnp.exp(m_prev - m_new)
            p = jnp.exp(s - m_new)
            l_ref[h] = alpha * l_ref[h] + jnp.sum(p, axis=1, keepdims=True)
            m_ref[h] = m_new
            vh = jnp.where(masks[h], v, jnp.zeros_like(v))
            acc_ref[h] = alpha * acc_ref[h] + jnp.dot(p.astype(BF16), vh, preferred_element_type=F32)

    tile(i, True)

    def body(t, carry):
        tile(i - 1 - t, False)
        return carry

    lax.fori_loop(0, i, body, 0)
    out = acc_ref[0] / l_ref[0]
    for h in range(1, HEADS_PER_GROUP):
        out = out + acc_ref[h] / l_ref[h]
    o_ref[...] = out.astype(o_ref.dtype)


def _fox_call(q, k, v, nf):
    s = q.shape[0]
    return pl.pallas_call(
        _fox_kernel,
        grid=(N_GROUPS, s // TQ),
        in_specs=[pl.BlockSpec((TQ, LANES), lambda g, i: (i, g)),
                  pl.BlockSpec((s, LANES), lambda g, i: (0, g)),
                  pl.BlockSpec((s, LANES), lambda g, i: (0, g)),
                  pl.BlockSpec((1, s // TK, HEADS_PER_GROUP, TK), lambda g, i: (g, 0, 0, 0))],
        out_specs=pl.BlockSpec((TQ, LANES), lambda g, i: (i, g)),
        out_shape=jax.ShapeDtypeStruct((s, D_BRANCH), BF16),
        scratch_shapes=[pltpu.VMEM((HEADS_PER_GROUP, TQ, LANES), F32),
                        pltpu.VMEM((HEADS_PER_GROUP, TQ, 1), F32),
                        pltpu.VMEM((HEADS_PER_GROUP, TQ, 1), F32)],
        compiler_params=pltpu.CompilerParams(dimension_semantics=("arbitrary", "arbitrary"),
                                             vmem_limit_bytes=VMEM_LIMIT),
        name="forgetting_attn",
    )(q, k, v, nf)


def _post_kernel(x_ref, osb_ref, ofx_ref, sh1_ref, sc1_ref, gt1_ref, sh2_ref, sc2_ref, gt2_ref,
                 gmix_ref, gmlp_ref, gfin_ref, wg_ref, bg_ref, wsb_ref, wfx_ref, wo_ref,
                 w1_ref, b1_ref, w2_ref, b2_ref, o_ref):
    d = x_ref.shape[1]
    x = x_ref[...]
    h1 = _rmsnorm(x, gmix_ref[...]) * (1.0 + sc1_ref[0:1, :]) + sh1_ref[0:1, :]
    h1b = h1.astype(BF16)
    branches = (jnp.dot(osb_ref[...], wsb_ref[...], preferred_element_type=F32),
                jnp.dot(ofx_ref[...], wfx_ref[...], preferred_element_type=F32))
    merged = None
    for b, proj in enumerate(branches):
        glogit = jnp.dot(h1b, wg_ref[:, b * d:(b + 1) * d], preferred_element_type=F32) \
            + bg_ref[:, b * d:(b + 1) * d]
        term = proj / (1.0 + jnp.exp(-glogit))
        merged = term if merged is None else merged + term
    x1 = x + gt1_ref[0:1, :] * jnp.dot(merged.astype(BF16), wo_ref[...], preferred_element_type=F32)

    h2 = _rmsnorm(x1, gmlp_ref[...]) * (1.0 + sc2_ref[0:1, :]) + sh2_ref[0:1, :]
    h2b = h2.astype(BF16)
    ff = None
    for c in range(w1_ref.shape[1] // FF_CHUNK):
        lo, hi = c * FF_CHUNK, (c + 1) * FF_CHUNK
        u = jnp.dot(h2b, w1_ref[:, lo:hi], preferred_element_type=F32) + b1_ref[:, lo:hi]
        u = jnp.square(jnp.maximum(u, 0.0))
        part = jnp.dot(u.astype(BF16), w2_ref[lo:hi, :], preferred_element_type=F32)
        ff = part if ff is None else ff + part
    x2 = x1 + gt2_ref[0:1, :] * (ff + b2_ref[...])
    o_ref[...] = _rmsnorm(x2, gfin_ref[...])


def _post_call(x2, o_sb, o_fx, mod, g_mix, g_mlp, g_final, w_gate, b_gate, w_out_sb, w_out_fox,
               w_o, w_ff1, b_ff1, w_ff2, b_ff2):
    s, d = x2.shape
    ts = TS_POST
    tok = lambda i: (i, 0)
    const = lambda i: (0, 0)

    def resident(a):
        return pl.BlockSpec(a.shape, const, pipeline_mode=pl.Buffered(1))

    mod_specs = [pl.BlockSpec((8, d), functools.partial(lambda k, i: (0, k), k)) for k in range(N_MOD)]
    return pl.pallas_call(
        _post_kernel,
        grid=(s // ts,),
        in_specs=[pl.BlockSpec((ts, d), tok),
                  pl.BlockSpec((ts, D_BRANCH), tok),
                  pl.BlockSpec((ts, D_BRANCH), tok)]
                 + mod_specs
                 + [resident(a) for a in (g_mix, g_mlp, g_final, w_gate, b_gate, w_out_sb, w_out_fox,
                                          w_o, w_ff1, b_ff1, w_ff2, b_ff2)],
        out_specs=pl.BlockSpec((ts, d), tok),
        out_shape=jax.ShapeDtypeStruct((s, d), F32),
        compiler_params=pltpu.CompilerParams(dimension_semantics=("arbitrary",),
                                             vmem_limit_bytes=VMEM_LIMIT),
        name="merge_mlp_norm",
    )(x2, o_sb, o_fx, *([mod] * N_MOD), g_mix, g_mlp, g_final, w_gate, b_gate, w_out_sb, w_out_fox,
      w_o, w_ff1, b_ff1, w_ff2, b_ff2)


def kernel(x, c, w_ada, b_ada, g_mix, w_in, b_f, w_gate, b_gate, w_out_sb, w_out_fox, w_o, g_mlp,
           w_ff1, b_ff1, w_ff2, b_ff2, g_final):
    batch, s, d = x.shape
    depth = w_ada.shape[0]
    n_qkv = 6 * D_BRANCH
    assert batch == 1 and c.shape == (1, d) and w_in.shape[2] == n_qkv + N_HEADS
    assert s % TS_PROJ == 0 and s % TS_POST == 0 and s % TQ == 0 and TQ == TK

    xs = x.reshape(s, d)
    c8 = jnp.broadcast_to(c, (8, d))
    row = lambda a: a.reshape(1, -1)
    for layer in range(depth):
        mod = _mod_call(c8, w_ada[layer], row(b_ada[layer]))
        w_qkv = w_in[layer, :, :n_qkv].astype(BF16)
        w_f = jnp.pad(w_in[layer, :, n_qkv:], ((0, 0), (0, LANES - N_HEADS)))
        b_f_row = jnp.pad(row(b_f[layer]), ((0, 0), (0, LANES - N_HEADS)))
        q_sb, k_sb, v_sb, q_fx, k_fx, v_fx, neg_f = _proj_call(xs, mod, row(g_mix[layer]), w_qkv, w_f, b_f_row)
        o_sb = _sb_call(q_sb, k_sb, v_sb)
        o_fx = _fox_call(q_fx, k_fx, v_fx, neg_f)
        is_last = layer == depth - 1
        assert is_last, "the fused final norm assumes a single layer"
        xs = _post_call(xs, o_sb, o_fx, mod, row(g_mix[layer]), row(g_mlp[layer]), row(g_final),
                        w_gate[layer].astype(BF16), row(b_gate[layer]),
                        w_out_sb[layer].astype(BF16), w_out_fox[layer].astype(BF16),
                        w_o[layer].astype(BF16), w_ff1[layer].astype(BF16), row(b_ff1[layer]),
                        w_ff2[layer].astype(BF16), row(b_ff2[layer]))
    return xs.reshape(batch, s, d)
```

```python
import functools

import jax
import jax.numpy as jnp
from jax import lax
from jax.experimental import pallas as pl
from jax.experimental.pallas import tpu as pltpu

F32 = jnp.float32
BF16 = jnp.bfloat16

HEAD_DIM = 64
N_HEADS = 8
D_BRANCH = N_HEADS * HEAD_DIM
N_MOD = 6
EPS = 1e-6

LANES = 128
HEADS_PER_GROUP = LANES // HEAD_DIM
N_GROUPS = D_BRANCH // LANES

TS_PROJ = 512
TS_POST = 512
TQ = 256
TK = 256
FOX_TQ = 512
FOX_TK = 256
FF_CHUNK = 1024
BIAS_TERMS = 3
ONES_ROWS = 16

STICK_DEAD = 104.0
NEG_BIG = -1e30
VMEM_LIMIT = 56 * 1024 * 1024

_NT = (((1,), (1,)), ((), ()))


def _rmsnorm(x, g):
    return x * lax.rsqrt(jnp.mean(x * x, axis=-1, keepdims=True) + EPS) * g


def _softplus(z):
    return jnp.maximum(z, 0.0) + jnp.log(1.0 + jnp.exp(-jnp.abs(z)))


def _mod_kernel(c_ref, w_ref, b_ref, o_ref):
    o_ref[...] = jnp.dot(c_ref[...], w_ref[...], preferred_element_type=F32,
                         precision=lax.Precision.HIGHEST) + b_ref[...]


def _mod_call(c8, w_ada, b_ada):
    d, n = w_ada.shape
    return pl.pallas_call(
        _mod_kernel,
        grid=(n // d,),
        in_specs=[pl.BlockSpec((8, d), lambda j: (0, 0)),
                  pl.BlockSpec((d, d), lambda j: (0, j)),
                  pl.BlockSpec((1, d), lambda j: (0, j))],
        out_specs=pl.BlockSpec((8, d), lambda j: (0, j)),
        out_shape=jax.ShapeDtypeStruct((8, n), F32),
        name="adaln_mod",
    )(c8, w_ada, b_ada)


def _split3(x):
    hi = x.astype(BF16)
    r = x - hi.astype(F32)
    mid = r.astype(BF16)
    lo = (r - mid.astype(F32)).astype(BF16)
    return hi, mid, lo


def _proj_kernel(x_ref, sh_ref, sc_ref, g_ref, w_ref, wt_ref, wf_ref, bf_ref,
                 qs_ref, ks_ref, vs_ref, qft_ref, kf_ref, vft_ref, carry_ref):
    ts = x_ref.shape[0]

    @pl.when(pl.program_id(0) == 0)
    def _():
        carry_ref[...] = jnp.zeros_like(carry_ref)

    h = _rmsnorm(x_ref[...], g_ref[...]) * (1.0 + sc_ref[0:1, :]) + sh_ref[0:1, :]
    hb = h.astype(BF16)
    scale = HEAD_DIM ** -0.5

    def proj(j):
        return jnp.dot(hb, w_ref[:, j * D_BRANCH:(j + 1) * D_BRANCH], preferred_element_type=F32)

    def proj_t(j):
        return lax.dot_general(wt_ref[j * D_BRANCH:(j + 1) * D_BRANCH, :], hb, _NT,
                               preferred_element_type=F32)

    qs_ref[...] = (proj(0) * scale).astype(BF16)
    ks_ref[...] = proj(1).astype(BF16)
    vs_ref[...] = proj(2).astype(BF16)
    qft_ref[...] = (proj_t(0) * scale).astype(BF16)
    vft_ref[...] = proj_t(1).astype(BF16)
    k_fox = proj(3).astype(BF16)

    logit = jnp.dot(h, wf_ref[...], preferred_element_type=F32,
                    precision=lax.Precision.HIGHEST) + bf_ref[...]
    log_f = jnp.minimum(logit, 0.0) - jnp.log(1.0 + jnp.exp(-jnp.abs(logit)))
    r = lax.broadcasted_iota(jnp.int32, (ts, ts), 0)
    c = lax.broadcasted_iota(jnp.int32, (ts, ts), 1)
    lower = jnp.where(c <= r, 1.0, 0.0).astype(F32)
    cum = jnp.dot(lower, log_f, preferred_element_type=F32,
                  precision=lax.Precision.HIGHEST) + carry_ref[0:1, :]
    carry_ref[...] = jnp.broadcast_to(cum[ts - 1:ts, :], carry_ref.shape)

    er = lax.broadcasted_iota(jnp.int32, (LANES, D_BRANCH), 0)
    ec = lax.broadcasted_iota(jnp.int32, (LANES, D_BRANCH), 1)
    target = LANES * (er // HEADS_PER_GROUP) + BIAS_TERMS * (er % HEADS_PER_GROUP)
    bias = None
    for t, part in enumerate(_split3(-cum)):
        place = jnp.where((ec == target + t) & (er < N_HEADS), 1.0, 0.0).astype(BF16)
        term = jnp.dot(part, place, preferred_element_type=F32)
        bias = term if bias is None else bias + term
    bias = bias.astype(BF16)
    for g in range(N_GROUPS):
        kf_ref[:, 2 * g * LANES:(2 * g + 1) * LANES] = k_fox[:, g * LANES:(g + 1) * LANES]
        kf_ref[:, (2 * g + 1) * LANES:(2 * g + 2) * LANES] = bias[:, g * LANES:(g + 1) * LANES]


def _proj_call(x2, mod, g_mix, w_nn, w_nt, w_f, b_f):
    s, d = x2.shape
    ts = TS_PROJ
    tok = lambda i: (i, 0)
    tok_t = lambda i: (0, i)
    const = lambda i: (0, 0)
    resident = lambda a: pl.BlockSpec(a.shape, const, pipeline_mode=pl.Buffered(1))
    return pl.pallas_call(
        _proj_kernel,
        grid=(s // ts,),
        in_specs=[pl.BlockSpec((ts, d), tok),
                  pl.BlockSpec((8, d), lambda i: (0, 0)),
                  pl.BlockSpec((8, d), lambda i: (0, 1)),
                  pl.BlockSpec((1, d), const),
                  resident(w_nn), resident(w_nt),
                  pl.BlockSpec((d, LANES), const),
                  pl.BlockSpec((1, LANES), const)],
        out_specs=[pl.BlockSpec((ts, D_BRANCH), tok)] * 3 + [
            pl.BlockSpec((D_BRANCH, ts), tok_t),
            pl.BlockSpec((ts, 2 * D_BRANCH), tok),
            pl.BlockSpec((D_BRANCH, ts), tok_t)],
        out_shape=[jax.ShapeDtypeStruct((s, D_BRANCH), BF16)] * 3 + [
            jax.ShapeDtypeStruct((D_BRANCH, s), BF16),
            jax.ShapeDtypeStruct((s, 2 * D_BRANCH), BF16),
            jax.ShapeDtypeStruct((D_BRANCH, s), BF16)],
        scratch_shapes=[pltpu.VMEM((8, LANES), F32)],
        compiler_params=pltpu.CompilerParams(dimension_semantics=("arbitrary",),
                                             vmem_limit_bytes=VMEM_LIMIT),
        name="proj",
    )(x2, mod, mod, g_mix, w_nn, w_nt, w_f, b_f)


def _head_masks():
    lane = lax.broadcasted_iota(jnp.int32, (1, LANES), 1)
    return [(lane >= h * HEAD_DIM) & (lane < (h + 1) * HEAD_DIM) for h in range(HEADS_PER_GROUP)]


def _tile_pos():
    row = lax.broadcasted_iota(jnp.int32, (TQ, TK), 0)
    col = lax.broadcasted_iota(jnp.int32, (TQ, TK), 1)
    return row, col


def _sb_kernel(q_ref, k_ref, v_ref, o_ref, acc_ref, carry_ref):
    i = pl.program_id(1)
    masks = _head_masks()
    q = q_ref[...]
    zero = jnp.zeros_like(q)
    qh = [jnp.where(m, q, zero) for m in masks]
    row, col = _tile_pos()
    strictly_before = col < row
    tri = jnp.where(row >= col, 1.0, 0.0).astype(BF16)

    acc_ref[...] = jnp.zeros_like(acc_ref)
    carry_ref[...] = jnp.zeros_like(carry_ref)

    def tile(j, diagonal):
        ks = pl.multiple_of(j * TK, TK)
        k = k_ref[pl.ds(ks, TK), :]
        v = v_ref[pl.ds(ks, TK), :]
        alive = None
        for h in range(HEADS_PER_GROUP):
            z = lax.dot_general(qh[h], k, _NT, preferred_element_type=F32)
            sp = _softplus(z)
            if diagonal:
                sp = jnp.where(strictly_before, sp, 0.0)
            sp_hi = sp.astype(BF16)
            sp_lo = (sp - sp_hi.astype(F32)).astype(BF16)
            csum = (jnp.dot(sp_hi, tri, preferred_element_type=F32)
                    + jnp.dot(sp_lo, tri, preferred_element_type=F32))
            carry = carry_ref[h]
            w = jnp.exp(z - csum - carry)
            if diagonal:
                w = jnp.where(strictly_before, w, 0.0)
            vh = jnp.where(masks[h], v, jnp.zeros_like(v))
            acc_ref[...] += jnp.dot(w.astype(BF16), vh, preferred_element_type=F32)
            new_carry = carry + csum[:, 0:1]
            carry_ref[h] = new_carry
            low = jnp.min(new_carry)
            alive = low if alive is None else jnp.minimum(alive, low)
        return alive

    low0 = tile(i, True)

    def cond(state):
        j, low = state
        return (j >= 0) & (low <= STICK_DEAD)

    def body(state):
        j, _ = state
        return j - 1, tile(j, False)

    lax.while_loop(cond, body, (i - 1, low0))
    o_ref[...] = acc_ref[...].astype(o_ref.dtype)


def _sb_call(q, k, v):
    s = q.shape[0]
    return pl.pallas_call(
        _sb_kernel,
        grid=(N_GROUPS, s // TQ),
        in_specs=[pl.BlockSpec((TQ, LANES), lambda g, i: (i, g)),
                  pl.BlockSpec((s, LANES), lambda g, i: (0, g)),
                  pl.BlockSpec((s, LANES), lambda g, i: (0, g))],
        out_specs=pl.BlockSpec((TQ, LANES), lambda g, i: (i, g)),
        out_shape=jax.ShapeDtypeStruct((s, D_BRANCH), BF16),
        scratch_shapes=[pltpu.VMEM((TQ, LANES), F32),
                        pltpu.VMEM((HEADS_PER_GROUP, TQ, 1), F32)],
        compiler_params=pltpu.CompilerParams(dimension_semantics=("arbitrary", "arbitrary"),
                                             vmem_limit_bytes=VMEM_LIMIT),
        name="stick_breaking_attn",
    )(q, k, v)


def _fox_kernel(qt_ref, k_ref, vt_ref, o_ref, acc_ref, m_ref):
    tq, tk = FOX_TQ, FOX_TK
    i = pl.program_id(1)
    qt = qt_ref[...]
    sub = lax.broadcasted_iota(jnp.int32, (LANES, tq), 0)
    rhs = []
    for h in range(HEADS_PER_GROUP):
        q_rows = jnp.where((sub >= h * HEAD_DIM) & (sub < (h + 1) * HEAD_DIM), qt, jnp.zeros_like(qt))
        one_rows = jnp.where((sub >= h * BIAS_TERMS) & (sub < (h + 1) * BIAS_TERMS), 1.0, 0.0).astype(BF16)
        rhs.append(jnp.concatenate([q_rows, one_rows], axis=0))
    key_pos = lax.broadcasted_iota(jnp.int32, (tk, tq), 0)
    query_pos = lax.broadcasted_iota(jnp.int32, (tk, tq), 1)
    ones = jnp.ones((ONES_ROWS, tk), BF16)

    acc_ref[...] = jnp.zeros_like(acc_ref)
    m_ref[...] = jnp.full_like(m_ref, NEG_BIG)

    def tile(j, key_offset):
        ks = pl.multiple_of(j * tk, tk)
        k = k_ref[pl.ds(ks, tk), :]
        vt = vt_ref[:, pl.ds(ks, tk)]
        scores = [jnp.dot(k, rhs[h], preferred_element_type=F32) for h in range(HEADS_PER_GROUP)]
        for h, s in enumerate(scores):
            if key_offset is not None:
                s = jnp.where(key_pos + key_offset <= query_pos, s, NEG_BIG)
            m_prev = m_ref[h]
            m_new = jnp.maximum(m_prev, jnp.max(s, axis=0, keepdims=True))
            alpha = jnp.exp(m_prev - m_new)
            p = jnp.exp(s - m_new).astype(BF16)
            lhs = jnp.concatenate([vt[h * HEAD_DIM:(h + 1) * HEAD_DIM, :], ones], axis=0)
            acc_ref[h] = alpha * acc_ref[h] + jnp.dot(lhs, p, preferred_element_type=F32)
            m_ref[h] = m_new

    n_diag = tq // tk
    for d in reversed(range(n_diag)):
        tile(i * n_diag + d, d * tk)

    def body(t, carry):
        tile(i * n_diag - 1 - t, None)
        return carry

    lax.fori_loop(0, i * n_diag, body, 0)
    outs = []
    for h in range(HEADS_PER_GROUP):
        acc = acc_ref[h]
        outs.append(acc[0:HEAD_DIM, :] / acc[HEAD_DIM:HEAD_DIM + 1, :])
    o_ref[...] = jnp.concatenate(outs, axis=0).T.astype(o_ref.dtype)


def _fox_call(qt, k, vt):
    s = k.shape[0]
    return pl.pallas_call(
        _fox_kernel,
        grid=(N_GROUPS, s // FOX_TQ),
        in_specs=[pl.BlockSpec((LANES, FOX_TQ), lambda g, i: (g, i)),
                  pl.BlockSpec((s, 2 * LANES), lambda g, i: (0, g)),
                  pl.BlockSpec((LANES, s), lambda g, i: (g, 0))],
        out_specs=pl.BlockSpec((FOX_TQ, LANES), lambda g, i: (i, g)),
        out_shape=jax.ShapeDtypeStruct((s, D_BRANCH), BF16),
        scratch_shapes=[pltpu.VMEM((HEADS_PER_GROUP, HEAD_DIM + ONES_ROWS, FOX_TQ), F32),
                        pltpu.VMEM((HEADS_PER_GROUP, 1, FOX_TQ), F32)],
        compiler_params=pltpu.CompilerParams(dimension_semantics=("arbitrary", "arbitrary"),
                                             vmem_limit_bytes=VMEM_LIMIT),
        name="forgetting_attn",
    )(qt, k, vt)


def _post_kernel(x_ref, osb_ref, ofx_ref, sh1_ref, sc1_ref, gt1_ref, sh2_ref, sc2_ref, gt2_ref,
                 gmix_ref, gmlp_ref, gfin_ref, wg_ref, bg_ref, wsb_ref, wfx_ref, wo_ref,
                 w1_ref, b1_ref, w2_ref, b2_ref, o_ref):
    d = x_ref.shape[1]
    x = x_ref[...]
    h1 = _rmsnorm(x, gmix_ref[...]) * (1.0 + sc1_ref[0:1, :]) + sh1_ref[0:1, :]
    h1b = h1.astype(BF16)
    branches = (jnp.dot(osb_ref[...], wsb_ref[...], preferred_element_type=F32),
                jnp.dot(ofx_ref[...], wfx_ref[...], preferred_element_type=F32))
    merged = None
    for b, proj in enumerate(branches):
        glogit = jnp.dot(h1b, wg_ref[:, b * d:(b + 1) * d], preferred_element_type=F32) \
            + bg_ref[:, b * d:(b + 1) * d]
        term = proj / (1.0 + jnp.exp(-glogit))
        merged = term if merged is None else merged + term
    x1 = x + gt1_ref[0:1, :] * jnp.dot(merged.astype(BF16), wo_ref[...], preferred_element_type=F32)

    h2 = _rmsnorm(x1, gmlp_ref[...]) * (1.0 + sc2_ref[0:1, :]) + sh2_ref[0:1, :]
    h2b = h2.astype(BF16)
    ff = None
    for c in range(w1_ref.shape[1] // FF_CHUNK):
        lo, hi = c * FF_CHUNK, (c + 1) * FF_CHUNK
        u = jnp.dot(h2b, w1_ref[:, lo:hi], preferred_element_type=F32) + b1_ref[:, lo:hi]
        u = jnp.square(jnp.maximum(u, 0.0))
        part = jnp.dot(u.astype(BF16), w2_ref[lo:hi, :], preferred_element_type=F32)
        ff = part if ff is None else ff + part
    x2 = x1 + gt2_ref[0:1, :] * (ff + b2_ref[...])
    o_ref[...] = _rmsnorm(x2, gfin_ref[...])


def _post_call(x2, o_sb, o_fx, mod, g_mix, g_mlp, g_final, w_gate, b_gate, w_out_sb, w_out_fox,
               w_o, w_ff1, b_ff1, w_ff2, b_ff2):
    s, d = x2.shape
    ts = TS_POST
    tok = lambda i: (i, 0)
    const = lambda i: (0, 0)

    def resident(a):
        return pl.BlockSpec(a.shape, const, pipeline_mode=pl.Buffered(1))

    mod_specs = [pl.BlockSpec((8, d), functools.partial(lambda k, i: (0, k), k)) for k in range(N_MOD)]
    return pl.pallas_call(
        _post_kernel,
        grid=(s // ts,),
        in_specs=[pl.BlockSpec((ts, d), tok),
                  pl.BlockSpec((ts, D_BRANCH), tok),
                  pl.BlockSpec((ts, D_BRANCH), tok)]
                 + mod_specs
                 + [resident(a) for a in (g_mix, g_mlp, g_final, w_gate, b_gate, w_out_sb, w_out_fox,
                                          w_o, w_ff1, b_ff1, w_ff2, b_ff2)],
        out_specs=pl.BlockSpec((ts, d), tok),
        out_shape=jax.ShapeDtypeStruct((s, d), F32),
        compiler_params=pltpu.CompilerParams(dimension_semantics=("arbitrary",),
                                             vmem_limit_bytes=VMEM_LIMIT),
        name="merge_mlp_norm",
    )(x2, o_sb, o_fx, *([mod] * N_MOD), g_mix, g_mlp, g_final, w_gate, b_gate, w_out_sb, w_out_fox,
      w_o, w_ff1, b_ff1, w_ff2, b_ff2)


def kernel(x, c, w_ada, b_ada, g_mix, w_in, b_f, w_gate, b_gate, w_out_sb, w_out_fox, w_o, g_mlp,
           w_ff1, b_ff1, w_ff2, b_ff2, g_final):
    batch, s, d = x.shape
    n_qkv = 6 * D_BRANCH
    assert w_ada.shape[0] == 1, "the final norm is fused into the single layer's last kernel"
    assert batch == 1 and c.shape == (1, d) and w_in.shape[2] == n_qkv + N_HEADS
    assert s % TS_PROJ == 0 and s % TS_POST == 0 and s % TQ == 0 and TQ == TK
    assert s % FOX_TQ == 0 and FOX_TQ % FOX_TK == 0

    xs = x.reshape(s, d)
    c8 = jnp.broadcast_to(c, (8, d))
    row = lambda a: a.reshape(1, -1)
    cols = lambda j: w_in[0, :, j * D_BRANCH:(j + 1) * D_BRANCH]

    mod = _mod_call(c8, w_ada[0], row(b_ada[0]))
    w_nn = jnp.concatenate([cols(0), cols(1), cols(2), cols(4)], axis=1).astype(BF16)
    w_nt = jnp.concatenate([cols(3), cols(5)], axis=1).T.astype(BF16)
    w_f = jnp.pad(w_in[0, :, n_qkv:], ((0, 0), (0, LANES - N_HEADS)))
    b_f_row = jnp.pad(row(b_f[0]), ((0, 0), (0, LANES - N_HEADS)))
    q_sb, k_sb, v_sb, qt_fx, k_fx, vt_fx = _proj_call(xs, mod, row(g_mix[0]), w_nn, w_nt, w_f, b_f_row)
    o_sb = _sb_call(q_sb, k_sb, v_sb)
    o_fx = _fox_call(qt_fx, k_fx, vt_fx)
    out = _post_call(xs, o_sb, o_fx, mod, row(g_mix[0]), row(g_mlp[0]), row(g_final),
                     w_gate[0].astype(BF16), row(b_gate[0]),
                     w_out_sb[0].astype(BF16), w_out_fox[0].astype(BF16),
                     w_o[0].astype(BF16), w_ff1[0].astype(BF16), row(b_ff1[0]),
                     w_ff2[0].astype(BF16), row(b_ff2[0]))
    return out.reshape(batch, s, d)
```

```python
import functools

import jax
import jax.numpy as jnp
from jax import lax
from jax.experimental import pallas as pl
from jax.experimental.pallas import tpu as pltpu

F32 = jnp.float32
BF16 = jnp.bfloat16

HEAD_DIM = 64
N_HEADS = 8
D_BRANCH = N_HEADS * HEAD_DIM
N_MOD = 6
EPS = 1e-6

LANES = 128
HEADS_PER_GROUP = LANES // HEAD_DIM
N_GROUPS = D_BRANCH // LANES

TS_PROJ = 512
TS_POST = 512
SB_T = 256
FOX_T = 512
FF_CHUNK = 1024
BIAS_TERMS = 3
ONES_ROWS = 16

STICK_DEAD = 104.0
FOX_DEAD = 105.0
NEG_BIG = -1e30
VMEM_LIMIT = 56 * 1024 * 1024

_NT = (((1,), (1,)), ((), ()))


def _rmsnorm(x, g):
    return x * lax.rsqrt(jnp.mean(x * x, axis=-1, keepdims=True) + EPS) * g


def _softplus(z):
    return jnp.maximum(z, 0.0) + jnp.log(1.0 + jnp.exp(-jnp.abs(z)))


def _split2(x):
    hi = x.astype(BF16)
    return hi, (x - hi.astype(F32)).astype(BF16)


def _split3(x):
    hi = x.astype(BF16)
    r = x - hi.astype(F32)
    mid = r.astype(BF16)
    return hi, mid, (r - mid.astype(F32)).astype(BF16)


def _head_rows(qt, h):
    sub = lax.broadcasted_iota(jnp.int32, qt.shape, 0)
    return jnp.where((sub >= h * HEAD_DIM) & (sub < (h + 1) * HEAD_DIM), qt, jnp.zeros_like(qt))


def _mod_kernel(c_ref, w_ref, b_ref, o_ref):
    o_ref[...] = jnp.dot(c_ref[...], w_ref[...], preferred_element_type=F32,
                         precision=lax.Precision.HIGHEST) + b_ref[...]


def _mod_call(c8, w_ada, b_ada):
    d, n = w_ada.shape
    return pl.pallas_call(
        _mod_kernel,
        grid=(n // d,),
        in_specs=[pl.BlockSpec((8, d), lambda j: (0, 0)),
                  pl.BlockSpec((d, d), lambda j: (0, j)),
                  pl.BlockSpec((1, d), lambda j: (0, j))],
        out_specs=pl.BlockSpec((8, d), lambda j: (0, j)),
        out_shape=jax.ShapeDtypeStruct((8, n), F32),
        name="adaln_mod",
    )(c8, w_ada, b_ada)


def _proj_kernel(x_ref, sh_ref, sc_ref, g_ref, w_ref, wt_ref, wf_ref, bf_ref,
                 qst_ref, ks_ref, vst_ref, qft_ref, kf_ref, vft_ref, nfe_ref, kmax_ref, carry_ref):
    ts = x_ref.shape[0]

    @pl.when(pl.program_id(0) == 0)
    def _():
        carry_ref[...] = jnp.zeros_like(carry_ref)
        kmax_ref[...] = jnp.zeros_like(kmax_ref)

    h = _rmsnorm(x_ref[...], g_ref[...]) * (1.0 + sc_ref[0:1, :]) + sh_ref[0:1, :]
    hb = h.astype(BF16)
    scale = HEAD_DIM ** -0.5

    def proj(j):
        return jnp.dot(hb, w_ref[:, j * D_BRANCH:(j + 1) * D_BRANCH], preferred_element_type=F32)

    def proj_t(j):
        return lax.dot_general(wt_ref[j * D_BRANCH:(j + 1) * D_BRANCH, :], hb, _NT,
                               preferred_element_type=F32)

    qst_ref[...] = (proj_t(0) * scale).astype(BF16)
    vst_ref[...] = proj_t(1).astype(BF16)
    qft_ref[...] = (proj_t(2) * scale).astype(BF16)
    vft_ref[...] = proj_t(3).astype(BF16)
    ks_ref[...] = proj(0).astype(BF16)
    k_fox = proj(1).astype(BF16)

    kf32 = k_fox.astype(F32)
    seg_r = lax.broadcasted_iota(jnp.int32, (D_BRANCH, LANES), 0)
    seg_c = lax.broadcasted_iota(jnp.int32, (D_BRANCH, LANES), 1)
    seg = jnp.where(seg_r // HEAD_DIM == seg_c, 1.0, 0.0).astype(BF16)
    sq_hi, sq_lo = _split2(kf32 * kf32)
    norm2 = (jnp.dot(sq_hi, seg, preferred_element_type=F32)
             + jnp.dot(sq_lo, seg, preferred_element_type=F32))
    block_max = jnp.max(jnp.sqrt(norm2), axis=0, keepdims=True)
    kmax_ref[...] = jnp.maximum(kmax_ref[...], jnp.broadcast_to(block_max, kmax_ref.shape))

    logit = jnp.dot(h, wf_ref[...], preferred_element_type=F32,
                    precision=lax.Precision.HIGHEST) + bf_ref[...]
    log_f = jnp.minimum(logit, 0.0) - jnp.log(1.0 + jnp.exp(-jnp.abs(logit)))
    r = lax.broadcasted_iota(jnp.int32, (ts, ts), 0)
    c = lax.broadcasted_iota(jnp.int32, (ts, ts), 1)
    lower = jnp.where(c <= r, 1.0, 0.0).astype(F32)
    cum = jnp.dot(lower, log_f, preferred_element_type=F32,
                  precision=lax.Precision.HIGHEST) + carry_ref[0:1, :]
    carry_ref[...] = jnp.broadcast_to(cum[ts - 1:ts, :], carry_ref.shape)
    neg_cum = -cum

    nfe_ref[...] = jnp.zeros_like(nfe_ref)
    for t in range(ts // FOX_T):
        nfe_ref[0, t:t + 1, :] = neg_cum[(t + 1) * FOX_T - 1:(t + 1) * FOX_T, :]

    er = lax.broadcasted_iota(jnp.int32, (LANES, D_BRANCH), 0)
    ec = lax.broadcasted_iota(jnp.int32, (LANES, D_BRANCH), 1)
    target = LANES * (er // HEADS_PER_GROUP) + BIAS_TERMS * (er % HEADS_PER_GROUP)
    bias = None
    for t, part in enumerate(_split3(neg_cum)):
        place = jnp.where((ec == target + t) & (er < N_HEADS), 1.0, 0.0).astype(BF16)
        term = jnp.dot(part, place, preferred_element_type=F32)
        bias = term if bias is None else bias + term
    bias = bias.astype(BF16)
    for g in range(N_GROUPS):
        kf_ref[:, 2 * g * LANES:(2 * g + 1) * LANES] = k_fox[:, g * LANES:(g + 1) * LANES]
        kf_ref[:, (2 * g + 1) * LANES:(2 * g + 2) * LANES] = bias[:, g * LANES:(g + 1) * LANES]


def _proj_call(x2, mod, g_mix, w_nn, w_nt, w_f, b_f):
    s, d = x2.shape
    ts = TS_PROJ
    tok = lambda i: (i, 0)
    tok_t = lambda i: (0, i)
    const = lambda i: (0, 0)
    resident = lambda a: pl.BlockSpec(a.shape, const, pipeline_mode=pl.Buffered(1))
    t_spec = pl.BlockSpec((D_BRANCH, ts), tok_t)
    t_shape = jax.ShapeDtypeStruct((D_BRANCH, s), BF16)
    return pl.pallas_call(
        _proj_kernel,
        grid=(s // ts,),
        in_specs=[pl.BlockSpec((ts, d), tok),
                  pl.BlockSpec((8, d), lambda i: (0, 0)),
                  pl.BlockSpec((8, d), lambda i: (0, 1)),
                  pl.BlockSpec((1, d), const),
                  resident(w_nn), resident(w_nt),
                  pl.BlockSpec((d, LANES), const),
                  pl.BlockSpec((1, LANES), const)],
        out_specs=[t_spec, pl.BlockSpec((ts, D_BRANCH), tok), t_spec,
                   t_spec, pl.BlockSpec((ts, 2 * D_BRANCH), tok), t_spec,
                   pl.BlockSpec((1, 8, LANES), lambda i: (i, 0, 0)),
                   pl.BlockSpec((8, LANES), const)],
        out_shape=[t_shape, jax.ShapeDtypeStruct((s, D_BRANCH), BF16), t_shape,
                   t_shape, jax.ShapeDtypeStruct((s, 2 * D_BRANCH), BF16), t_shape,
                   jax.ShapeDtypeStruct((s // ts, 8, LANES), F32),
                   jax.ShapeDtypeStruct((8, LANES), F32)],
        scratch_shapes=[pltpu.VMEM((8, LANES), F32)],
        compiler_params=pltpu.CompilerParams(dimension_semantics=("arbitrary",),
                                             vmem_limit_bytes=VMEM_LIMIT),
        name="proj",
    )(x2, mod, mod, g_mix, w_nn, w_nt, w_f, b_f)


def _sb_kernel(qt_ref, k_ref, vt_ref, o_ref, acc_ref, carry_ref):
    t = SB_T
    i = pl.program_id(1)
    qt = qt_ref[...]
    rhs = [_head_rows(qt, h) for h in range(HEADS_PER_GROUP)]
    key_pos = lax.broadcasted_iota(jnp.int32, (t, t), 0)
    query_pos = lax.broadcasted_iota(jnp.int32, (t, t), 1)
    before = key_pos < query_pos
    tri = jnp.where(query_pos >= key_pos, 1.0, 0.0).astype(BF16)

    def scores(j):
        k = k_ref[pl.ds(pl.multiple_of(j * t, t), t), :]
        return [jnp.dot(k, rhs[h], preferred_element_type=F32) for h in range(HEADS_PER_GROUP)]

    def suffix_mass(z, diagonal):
        sp = _softplus(z)
        if diagonal:
            sp = jnp.where(before, sp, 0.0)
        hi, lo = _split2(sp)
        return jnp.dot(tri, hi, preferred_element_type=F32) + jnp.dot(tri, lo, preferred_element_type=F32)

    def weights(z, mass, carry, diagonal):
        w = jnp.exp(z - mass - carry)
        if diagonal:
            w = jnp.where(before, w, 0.0)
        return w.astype(BF16)

    def values(j):
        return vt_ref[:, pl.ds(pl.multiple_of(j * t, t), t)]

    def head_values(vt, h):
        return vt[h * HEAD_DIM:(h + 1) * HEAD_DIM, :]

    @pl.when(i == 0)
    def _():
        vt = values(0)
        for h, z in enumerate(scores(0)):
            mass = suffix_mass(z, True)
            acc_ref[h] = jnp.dot(head_values(vt, h), weights(z, mass, 0.0, True), preferred_element_type=F32)
            carry_ref[h] = mass[0:1, :]

    @pl.when(i > 0)
    def _():
        z_diag, z_prev = scores(i), scores(i - 1)
        vt_diag, vt_prev = values(i), values(i - 1)
        for h in range(HEADS_PER_GROUP):
            mass_diag = suffix_mass(z_diag[h], True)
            mass_prev = suffix_mass(z_prev[h], False)
            carry = mass_diag[0:1, :]
            acc_ref[h] = (
                jnp.dot(head_values(vt_diag, h), weights(z_diag[h], mass_diag, 0.0, True),
                        preferred_element_type=F32)
                + jnp.dot(head_values(vt_prev, h), weights(z_prev[h], mass_prev, carry, False),
                          preferred_element_type=F32))
            carry_ref[h] = carry + mass_prev[0:1, :]

    def cond(state):
        j, low = state
        return (j >= 0) & (low <= STICK_DEAD)

    def body(state):
        j, _ = state
        vt = values(j)
        for h, z in enumerate(scores(j)):
            mass = suffix_mass(z, False)
            carry = carry_ref[h]
            acc_ref[h] += jnp.dot(head_values(vt, h), weights(z, mass, carry, False),
                                  preferred_element_type=F32)
            carry_ref[h] = carry + mass[0:1, :]
        return j - 1, jnp.min(carry_ref[...])

    lax.while_loop(cond, body, (i - 2, jnp.min(carry_ref[...])))
    out = jnp.concatenate([acc_ref[h] for h in range(HEADS_PER_GROUP)], axis=0)
    o_ref[...] = out.T.astype(o_ref.dtype)


def _sb_call(qt, k, vt):
    s = k.shape[0]
    t = SB_T
    return pl.pallas_call(
        _sb_kernel,
        grid=(N_GROUPS, s // t),
        in_specs=[pl.BlockSpec((LANES, t), lambda g, i: (g, i)),
                  pl.BlockSpec((s, LANES), lambda g, i: (0, g)),
                  pl.BlockSpec((LANES, s), lambda g, i: (g, 0))],
        out_specs=pl.BlockSpec((t, LANES), lambda g, i: (i, g)),
        out_shape=jax.ShapeDtypeStruct((s, D_BRANCH), BF16),
        scratch_shapes=[pltpu.VMEM((HEADS_PER_GROUP, HEAD_DIM, t), F32),
                        pltpu.VMEM((HEADS_PER_GROUP, 1, t), F32)],
        compiler_params=pltpu.CompilerParams(dimension_semantics=("arbitrary", "arbitrary"),
                                             vmem_limit_bytes=VMEM_LIMIT),
        name="stick_breaking_attn",
    )(qt, k, vt)


def _fox_kernel(nfe_ref, kmax_ref, qt_ref, k_ref, vt_ref, o_ref, acc_ref, m_ref):
    t = FOX_T
    g = pl.program_id(0)
    i = pl.program_id(1)
    qt = qt_ref[...]
    sub = lax.broadcasted_iota(jnp.int32, (LANES, t), 0)
    rhs = []
    for h in range(HEADS_PER_GROUP):
        one_rows = jnp.where((sub >= h * BIAS_TERMS) & (sub < (h + 1) * BIAS_TERMS), 1.0, 0.0).astype(BF16)
        rhs.append(jnp.concatenate([_head_rows(qt, h), one_rows], axis=0))
    key_pos = lax.broadcasted_iota(jnp.int32, (t, t), 0)
    query_pos = lax.broadcasted_iota(jnp.int32, (t, t), 1)
    not_after = key_pos <= query_pos
    ones = jnp.ones((ONES_ROWS, t), BF16)

    def tile(j, diagonal, first):
        ks = pl.multiple_of(j * t, t)
        k = k_ref[pl.ds(ks, t), :]
        vt = vt_ref[:, pl.ds(ks, t)]
        scores = [jnp.dot(k, rhs[h], preferred_element_type=F32) for h in range(HEADS_PER_GROUP)]
        for h, s in enumerate(scores):
            if diagonal:
                s = jnp.where(not_after, s, NEG_BIG)
            lhs = jnp.concatenate([vt[h * HEAD_DIM:(h + 1) * HEAD_DIM, :], ones], axis=0)
            if first:
                m_new = jnp.max(s, axis=0, keepdims=True)
                p = jnp.exp(s - m_new).astype(BF16)
                acc_ref[h] = jnp.dot(lhs, p, preferred_element_type=F32)
            else:
                m_prev = m_ref[h]
                m_new = jnp.maximum(m_prev, jnp.max(s, axis=0, keepdims=True))
                p = jnp.exp(s - m_new).astype(BF16)
                acc_ref[h] = jnp.exp(m_prev - m_new) * acc_ref[h] + jnp.dot(lhs, p, preferred_element_type=F32)
            m_ref[h] = m_new

    tile(i, True, True)

    qf = qt.astype(F32)
    slack = []
    for h in range(HEADS_PER_GROUP):
        qh = qf[h * HEAD_DIM:(h + 1) * HEAD_DIM, :]
        q_norm = jnp.sqrt(jnp.sum(qh * qh, axis=0, keepdims=True))
        slack.append(jnp.max(q_norm * kmax_ref[g * HEADS_PER_GROUP + h] - m_ref[h]))

    def cond(j):
        jj = jnp.maximum(j, 0)
        live = None
        for h in range(HEADS_PER_GROUP):
            alive = slack[h] + nfe_ref[jj * N_HEADS + g * HEADS_PER_GROUP + h] > -FOX_DEAD
            live = alive if live is None else live | alive
        return (j >= 0) & live

    def body(j):
        tile(j, False, False)
        return j - 1

    lax.while_loop(cond, body, i - 1)
    outs = []
    for h in range(HEADS_PER_GROUP):
        acc = acc_ref[h]
        outs.append(acc[0:HEAD_DIM, :] / acc[HEAD_DIM:HEAD_DIM + 1, :])
    o_ref[...] = jnp.concatenate(outs, axis=0).T.astype(o_ref.dtype)


def _fox_call(nfe, kmax, qt, k, vt):
    s = k.shape[0]
    t = FOX_T
    smem = pl.BlockSpec(memory_space=pltpu.SMEM)
    return pl.pallas_call(
        _fox_kernel,
        grid=(N_GROUPS, s // t),
        in_specs=[smem, smem,
                  pl.BlockSpec((LANES, t), lambda g, i: (g, i)),
                  pl.BlockSpec((s, 2 * LANES), lambda g, i: (0, g)),
                  pl.BlockSpec((LANES, s), lambda g, i: (g, 0))],
        out_specs=pl.BlockSpec((t, LANES), lambda g, i: (i, g)),
        out_shape=jax.ShapeDtypeStruct((s, D_BRANCH), BF16),
        scratch_shapes=[pltpu.VMEM((HEADS_PER_GROUP, HEAD_DIM + ONES_ROWS, t), F32),
                        pltpu.VMEM((HEADS_PER_GROUP, 1, t), F32)],
        compiler_params=pltpu.CompilerParams(dimension_semantics=("arbitrary", "arbitrary"),
                                             vmem_limit_bytes=VMEM_LIMIT),
        name="forgetting_attn",
    )(nfe, kmax, qt, k, vt)


def _post_kernel(x_ref, osb_ref, ofx_ref, sh1_ref, sc1_ref, gt1_ref, sh2_ref, sc2_ref, gt2_ref,
                 gmix_ref, gmlp_ref, gfin_ref, wg_ref, bg_ref, wsb_ref, wfx_ref, wo_ref,
                 w1_ref, b1_ref, w2_ref, b2_ref, o_ref):
    d = x_ref.shape[1]
    x = x_ref[...]
    h1 = _rmsnorm(x, gmix_ref[...]) * (1.0 + sc1_ref[0:1, :]) + sh1_ref[0:1, :]
    h1b = h1.astype(BF16)
    branches = (jnp.dot(osb_ref[...], wsb_ref[...], preferred_element_type=F32),
                jnp.dot(ofx_ref[...], wfx_ref[...], preferred_element_type=F32))
    merged = None
    for b, proj in enumerate(branches):
        glogit = jnp.dot(h1b, wg_ref[:, b * d:(b + 1) * d], preferred_element_type=F32) \
            + bg_ref[:, b * d:(b + 1) * d]
        term = proj / (1.0 + jnp.exp(-glogit))
        merged = term if merged is None else merged + term
    x1 = x + gt1_ref[0:1, :] * jnp.dot(merged.astype(BF16), wo_ref[...], preferred_element_type=F32)

    h2 = _rmsnorm(x1, gmlp_ref[...]) * (1.0 + sc2_ref[0:1, :]) + sh2_ref[0:1, :]
    h2b = h2.astype(BF16)
    ff = None
    for c in range(w1_ref.shape[1] // FF_CHUNK):
        lo, hi = c * FF_CHUNK, (c + 1) * FF_CHUNK
        u = jnp.dot(h2b, w1_ref[:, lo:hi], preferred_element_type=F32) + b1_ref[:, lo:hi]
        u = jnp.square(jnp.maximum(u, 0.0))
        part = jnp.dot(u.astype(BF16), w2_ref[lo:hi, :], preferred_element_type=F32)
        ff = part if ff is None else ff + part
    x2 = x1 + gt2_ref[0:1, :] * (ff + b2_ref[...])
    o_ref[...] = _rmsnorm(x2, gfin_ref[...])


def _post_call(x2, o_sb, o_fx, mod, g_mix, g_mlp, g_final, w_gate, b_gate, w_out_sb, w_out_fox,
               w_o, w_ff1, b_ff1, w_ff2, b_ff2):
    s, d = x2.shape
    ts = TS_POST
    tok = lambda i: (i, 0)
    const = lambda i: (0, 0)

    def resident(a):
        return pl.BlockSpec(a.shape, const, pipeline_mode=pl.Buffered(1))

    mod_specs = [pl.BlockSpec((8, d), functools.partial(lambda k, i: (0, k), k)) for k in range(N_MOD)]
    return pl.pallas_call(
        _post_kernel,
        grid=(s // ts,),
        in_specs=[pl.BlockSpec((ts, d), tok),
                  pl.BlockSpec((ts, D_BRANCH), tok),
                  pl.BlockSpec((ts, D_BRANCH), tok)]
                 + mod_specs
                 + [resident(a) for a in (g_mix, g_mlp, g_final, w_gate, b_gate, w_out_sb, w_out_fox,
                                          w_o, w_ff1, b_ff1, w_ff2, b_ff2)],
        out_specs=pl.BlockSpec((ts, d), tok),
        out_shape=jax.ShapeDtypeStruct((s, d), F32),
        compiler_params=pltpu.CompilerParams(dimension_semantics=("arbitrary",),
                                             vmem_limit_bytes=VMEM_LIMIT),
        name="merge_mlp_norm",
    )(x2, o_sb, o_fx, *([mod] * N_MOD), g_mix, g_mlp, g_final, w_gate, b_gate, w_out_sb, w_out_fox,
      w_o, w_ff1, b_ff1, w_ff2, b_ff2)


def kernel(x, c, w_ada, b_ada, g_mix, w_in, b_f, w_gate, b_gate, w_out_sb, w_out_fox, w_o, g_mlp,
           w_ff1, b_ff1, w_ff2, b_ff2, g_final):
    batch, s, d = x.shape
    n_qkv = 6 * D_BRANCH
    assert w_ada.shape[0] == 1, "the final norm is fused into the single layer's last kernel"
    assert batch == 1 and c.shape == (1, d) and w_in.shape[2] == n_qkv + N_HEADS
    assert s % TS_PROJ == 0 and s % TS_POST == 0 and s % SB_T == 0 and s % FOX_T == 0
    assert TS_PROJ % FOX_T == 0 and TS_PROJ // FOX_T <= 8

    xs = x.reshape(s, d)
    c8 = jnp.broadcast_to(c, (8, d))
    row = lambda a: a.reshape(1, -1)
    cols = lambda j: w_in[0, :, j * D_BRANCH:(j + 1) * D_BRANCH]

    mod = _mod_call(c8, w_ada[0], row(b_ada[0]))
    w_nn = jnp.concatenate([cols(1), cols(4)], axis=1).astype(BF16)
    w_nt = jnp.concatenate([cols(0), cols(2), cols(3), cols(5)], axis=1).T.astype(BF16)
    w_f = jnp.pad(w_in[0, :, n_qkv:], ((0, 0), (0, LANES - N_HEADS)))
    b_f_row = jnp.pad(row(b_f[0]), ((0, 0), (0, LANES - N_HEADS)))
    qt_sb, k_sb, vt_sb, qt_fx, k_fx, vt_fx, nf_end, k_max = _proj_call(
        xs, mod, row(g_mix[0]), w_nn, w_nt, w_f, b_f_row)
    o_sb = _sb_call(qt_sb, k_sb, vt_sb)
    nfe = nf_end[:, :TS_PROJ // FOX_T, :N_HEADS].reshape(-1)
    o_fx = _fox_call(nfe, k_max[0, :N_HEADS], qt_fx, k_fx, vt_fx)
    out = _post_call(xs, o_sb, o_fx, mod, row(g_mix[0]), row(g_mlp[0]), row(g_final),
                     w_gate[0].astype(BF16), row(b_gate[0]),
                     w_out_sb[0].astype(BF16), w_out_fox[0].astype(BF16),
                     w_o[0].astype(BF16), w_ff1[0].astype(BF16), row(b_ff1[0]),
                     w_ff2[0].astype(BF16), row(b_ff2[0]))
    return out.reshape(batch, s, d)
```

```python
import functools

import jax
import jax.numpy as jnp
from jax import lax
from jax.experimental import pallas as pl
from jax.experimental.pallas import tpu as pltpu

F32 = jnp.float32
BF16 = jnp.bfloat16

HEAD_DIM = 64
N_HEADS = 8
D_BRANCH = N_HEADS * HEAD_DIM
N_MOD = 6
EPS = 1e-6

LANES = 128
HEADS_PER_GROUP = LANES // HEAD_DIM
N_GROUPS = D_BRANCH // LANES

TS_PROJ = 512
TS_POST = 512
SB_T = 256
FOX_T = 512
FF_CHUNK = 1024
BIAS_TERMS = 3
ONES_ROWS = 16

LOG2E = 1.4426950408889634
Q_SCALE = HEAD_DIM ** -0.5 * LOG2E
STICK_DEAD = 104.0 * LOG2E
FOX_DEAD = 105.0 * LOG2E
NEG_BIG = -1e30
VMEM_LIMIT = 56 * 1024 * 1024

_NT = (((1,), (1,)), ((), ()))


def _rmsnorm(x, g):
    return x * lax.rsqrt(jnp.mean(x * x, axis=-1, keepdims=True) + EPS) * g


def _softplus2(y):
    return jnp.maximum(y, 0.0) + jnp.log2(1.0 + jnp.exp2(-jnp.abs(y)))


def _split2(x):
    hi = x.astype(BF16)
    return hi, (x - hi.astype(F32)).astype(BF16)


def _split3(x):
    hi = x.astype(BF16)
    r = x - hi.astype(F32)
    mid = r.astype(BF16)
    return hi, mid, (r - mid.astype(F32)).astype(BF16)


def _head_rows(qt, h):
    sub = lax.broadcasted_iota(jnp.int32, qt.shape, 0)
    return jnp.where((sub >= h * HEAD_DIM) & (sub < (h + 1) * HEAD_DIM), qt, jnp.zeros_like(qt))


def _mod_kernel(c_ref, w_ref, b_ref, o_ref):
    o_ref[...] = jnp.dot(c_ref[...], w_ref[...], preferred_element_type=F32,
                         precision=lax.Precision.HIGHEST) + b_ref[...]


def _mod_call(c8, w_ada, b_ada):
    d, n = w_ada.shape
    return pl.pallas_call(
        _mod_kernel,
        grid=(n // d,),
        in_specs=[pl.BlockSpec((8, d), lambda j: (0, 0)),
                  pl.BlockSpec((d, d), lambda j: (0, j)),
                  pl.BlockSpec((1, d), lambda j: (0, j))],
        out_specs=pl.BlockSpec((8, d), lambda j: (0, j)),
        out_shape=jax.ShapeDtypeStruct((8, n), F32),
        name="adaln_mod",
    )(c8, w_ada, b_ada)


def _proj_kernel(x_ref, sh_ref, sc_ref, g_ref, w_ref, wt_ref, bf_ref,
                 qst_ref, ks_ref, vst_ref, qft_ref, kf_ref, vft_ref, nfe_ref, kmax_ref, carry_ref):
    ts = x_ref.shape[0]
    reps = ts // LANES

    @pl.when(pl.program_id(0) == 0)
    def _():
        carry_ref[...] = jnp.zeros_like(carry_ref)
        kmax_ref[...] = jnp.zeros_like(kmax_ref)

    h = _rmsnorm(x_ref[...], g_ref[...]) * (1.0 + sc_ref[0:1, :]) + sh_ref[0:1, :]
    hb = h.astype(BF16)

    def proj(j):
        return jnp.dot(hb, w_ref[:, j * D_BRANCH:(j + 1) * D_BRANCH], preferred_element_type=F32)

    def proj_t(lo, hi):
        return lax.dot_general(wt_ref[lo:hi, :], hb, _NT, preferred_element_type=F32)

    qst_ref[...] = (proj_t(0, D_BRANCH) * Q_SCALE).astype(BF16)
    vst_ref[...] = proj_t(D_BRANCH, 2 * D_BRANCH).astype(BF16)
    qft_ref[...] = (proj_t(2 * D_BRANCH, 3 * D_BRANCH) * Q_SCALE).astype(BF16)
    vft_ref[...] = proj_t(3 * D_BRANCH, 4 * D_BRANCH).astype(BF16)
    ks_ref[...] = proj(0).astype(BF16)
    k_fox = proj(1).astype(BF16)

    kf32 = k_fox.astype(F32)
    seg_r = lax.broadcasted_iota(jnp.int32, (D_BRANCH, LANES), 0)
    seg_c = lax.broadcasted_iota(jnp.int32, (D_BRANCH, LANES), 1)
    seg = jnp.where(seg_r // HEAD_DIM == seg_c, 1.0, 0.0).astype(BF16)
    sq_hi, sq_lo = _split2(kf32 * kf32)
    norm2 = (jnp.dot(sq_hi, seg, preferred_element_type=F32)
             + jnp.dot(sq_lo, seg, preferred_element_type=F32))
    block_max = jnp.max(jnp.sqrt(norm2), axis=0, keepdims=True)
    kmax_ref[...] = jnp.maximum(kmax_ref[...], jnp.broadcast_to(block_max, kmax_ref.shape))

    along = lambda a: jnp.concatenate([a] * reps, axis=1)
    rows = proj_t(4 * D_BRANCH, 4 * D_BRANCH + 2 * N_HEADS)
    logit = rows[0:N_HEADS] + rows[N_HEADS:2 * N_HEADS] + along(bf_ref[...])
    log_f = jnp.minimum(logit, 0.0) - jnp.log(1.0 + jnp.exp(-jnp.abs(logit)))

    r = lax.broadcasted_iota(jnp.int32, (ts, ts), 0)
    c = lax.broadcasted_iota(jnp.int32, (ts, ts), 1)
    upper = jnp.where(r <= c, 1.0, 0.0).astype(BF16)

    def stack(parts, n_rows):
        pad = jnp.zeros((n_rows - BIAS_TERMS * N_HEADS, ts), F32)
        return jnp.concatenate([p.astype(F32) for p in parts] + [pad], axis=0)

    run = jnp.dot(stack(_split3(log_f), 4 * N_HEADS).astype(BF16), upper, preferred_element_type=F32)
    cum = (run[0:N_HEADS] + run[N_HEADS:2 * N_HEADS] + run[2 * N_HEADS:3 * N_HEADS]
           + along(carry_ref[...]))
    carry_ref[...] = jnp.broadcast_to(cum[:, ts - 1:ts], carry_ref.shape)
    neg_cum = cum * (-LOG2E)

    nfe_ref[0] = jnp.broadcast_to(neg_cum[:, ts - 1:ts], (N_HEADS, LANES))

    terms = stack(_split3(neg_cum), LANES).T.astype(BF16)
    er = lax.broadcasted_iota(jnp.int32, (LANES, D_BRANCH), 0)
    ec = lax.broadcasted_iota(jnp.int32, (LANES, D_BRANCH), 1)
    head, term = er % N_HEADS, er // N_HEADS
    target = LANES * (head // HEADS_PER_GROUP) + BIAS_TERMS * (head % HEADS_PER_GROUP) + term
    place = jnp.where((ec == target) & (term < BIAS_TERMS), 1.0, 0.0).astype(BF16)
    bias = jnp.dot(terms, place, preferred_element_type=F32).astype(BF16)
    for g in range(N_GROUPS):
        kf_ref[:, 2 * g * LANES:(2 * g + 1) * LANES] = k_fox[:, g * LANES:(g + 1) * LANES]
        kf_ref[:, (2 * g + 1) * LANES:(2 * g + 2) * LANES] = bias[:, g * LANES:(g + 1) * LANES]


def _proj_call(x2, mod, g_mix, w_nn, w_nt, b_f):
    s, d = x2.shape
    ts = TS_PROJ
    tok = lambda i: (i, 0)
    tok_t = lambda i: (0, i)
    const = lambda i: (0, 0)
    resident = lambda a: pl.BlockSpec(a.shape, const, pipeline_mode=pl.Buffered(1))
    t_spec = pl.BlockSpec((D_BRANCH, ts), tok_t)
    t_shape = jax.ShapeDtypeStruct((D_BRANCH, s), BF16)
    return pl.pallas_call(
        _proj_kernel,
        grid=(s // ts,),
        in_specs=[pl.BlockSpec((ts, d), tok),
                  pl.BlockSpec((8, d), lambda i: (0, 0)),
                  pl.BlockSpec((8, d), lambda i: (0, 1)),
                  pl.BlockSpec((1, d), const),
                  resident(w_nn), resident(w_nt),
                  pl.BlockSpec((N_HEADS, LANES), const)],
        out_specs=[t_spec, pl.BlockSpec((ts, D_BRANCH), tok), t_spec,
                   t_spec, pl.BlockSpec((ts, 2 * D_BRANCH), tok), t_spec,
                   pl.BlockSpec((1, N_HEADS, LANES), lambda i: (i, 0, 0)),
                   pl.BlockSpec((8, LANES), const)],
        out_shape=[t_shape, jax.ShapeDtypeStruct((s, D_BRANCH), BF16), t_shape,
                   t_shape, jax.ShapeDtypeStruct((s, 2 * D_BRANCH), BF16), t_shape,
                   jax.ShapeDtypeStruct((s // ts, N_HEADS, LANES), F32),
                   jax.ShapeDtypeStruct((8, LANES), F32)],
        scratch_shapes=[pltpu.VMEM((N_HEADS, LANES), F32)],
        compiler_params=pltpu.CompilerParams(dimension_semantics=("arbitrary",),
                                             vmem_limit_bytes=VMEM_LIMIT),
        name="proj",
    )(x2, mod, mod, g_mix, w_nn, w_nt, b_f)


def _sb_kernel(qt_ref, k_ref, vt_ref, o_ref, acc_ref, carry_ref):
    t = SB_T
    i = pl.program_id(1)
    qt = qt_ref[...]
    rhs = [_head_rows(qt, h) for h in range(HEADS_PER_GROUP)]
    key_pos = lax.broadcasted_iota(jnp.int32, (t, t), 0)
    query_pos = lax.broadcasted_iota(jnp.int32, (t, t), 1)
    before = key_pos < query_pos
    tri = jnp.where(query_pos >= key_pos, 1.0, 0.0).astype(BF16)

    def scores(j, diagonal):
        k = k_ref[pl.ds(pl.multiple_of(j * t, t), t), :]
        ys = [jnp.dot(k, rhs[h], preferred_element_type=F32) for h in range(HEADS_PER_GROUP)]
        if diagonal:
            ys = [jnp.where(before, y, NEG_BIG) for y in ys]
        return ys

    def suffix_mass(y):
        hi, lo = _split2(_softplus2(y))
        return jnp.dot(tri, hi, preferred_element_type=F32) + jnp.dot(tri, lo, preferred_element_type=F32)

    def values(j):
        return vt_ref[:, pl.ds(pl.multiple_of(j * t, t), t)]

    def weighted_values(vt, h, y, mass):
        w = jnp.exp2(y - mass).astype(BF16)
        return jnp.dot(vt[h * HEAD_DIM:(h + 1) * HEAD_DIM, :], w, preferred_element_type=F32)

    @pl.when(i == 0)
    def _():
        vt = values(0)
        for h, y in enumerate(scores(0, True)):
            mass = suffix_mass(y)
            acc_ref[h] = weighted_values(vt, h, y, mass)
            carry_ref[h] = mass[0:1, :]

    @pl.when(i > 0)
    def _():
        y_diag, y_prev = scores(i, True), scores(i - 1, False)
        vt_diag, vt_prev = values(i), values(i - 1)
        for h in range(HEADS_PER_GROUP):
            mass_diag = suffix_mass(y_diag[h])
            mass_prev = suffix_mass(y_prev[h])
            carry = mass_diag[0:1, :]
            acc_ref[h] = (weighted_values(vt_diag, h, y_diag[h], mass_diag)
                          + jnp.exp2(-carry) * weighted_values(vt_prev, h, y_prev[h], mass_prev))
            carry_ref[h] = carry + mass_prev[0:1, :]

    def cond(state):
        j, low = state
        return (j >= 0) & (low <= STICK_DEAD)

    def body(state):
        j, _ = state
        vt = values(j)
        for h, y in enumerate(scores(j, False)):
            mass = suffix_mass(y)
            carry = carry_ref[h]
            acc_ref[h] += jnp.exp2(-carry) * weighted_values(vt, h, y, mass)
            carry_ref[h] = carry + mass[0:1, :]
        return j - 1, jnp.min(carry_ref[...])

    lax.while_loop(cond, body, (i - 2, jnp.min(carry_ref[...])))
    out = jnp.concatenate([acc_ref[h] for h in range(HEADS_PER_GROUP)], axis=0)
    o_ref[...] = out.T.astype(o_ref.dtype)


def _sb_call(qt, k, vt):
    s = k.shape[0]
    t = SB_T
    return pl.pallas_call(
        _sb_kernel,
        grid=(N_GROUPS, s // t),
        in_specs=[pl.BlockSpec((LANES, t), lambda g, i: (g, i)),
                  pl.BlockSpec((s, LANES), lambda g, i: (0, g)),
                  pl.BlockSpec((LANES, s), lambda g, i: (g, 0))],
        out_specs=pl.BlockSpec((t, LANES), lambda g, i: (i, g)),
        out_shape=jax.ShapeDtypeStruct((s, D_BRANCH), BF16),
        scratch_shapes=[pltpu.VMEM((HEADS_PER_GROUP, HEAD_DIM, t), F32),
                        pltpu.VMEM((HEADS_PER_GROUP, 1, t), F32)],
        compiler_params=pltpu.CompilerParams(dimension_semantics=("arbitrary", "arbitrary"),
                                             vmem_limit_bytes=VMEM_LIMIT),
        name="stick_breaking_attn",
    )(qt, k, vt)


def _fox_kernel(nfe_ref, kmax_ref, qt_ref, k_ref, vt_ref, o_ref, acc_ref, m_ref, cur_ref, nxt_ref):
    t = FOX_T
    g = pl.program_id(0)
    i = pl.program_id(1)
    qt = qt_ref[...]
    sub = lax.broadcasted_iota(jnp.int32, (LANES, t), 0)
    rhs = []
    for h in range(HEADS_PER_GROUP):
        one_rows = jnp.where((sub >= h * BIAS_TERMS) & (sub < (h + 1) * BIAS_TERMS), 1.0, 0.0).astype(BF16)
        rhs.append(jnp.concatenate([_head_rows(qt, h), one_rows], axis=0))
    key_pos = lax.broadcasted_iota(jnp.int32, (t, t), 0)
    query_pos = lax.broadcasted_iota(jnp.int32, (t, t), 1)
    not_after = key_pos <= query_pos
    ones = jnp.ones((ONES_ROWS, t), BF16)

    def scores(j):
        k = k_ref[pl.ds(pl.multiple_of(j * t, t), t), :]
        return [jnp.dot(k, rhs[h], preferred_element_type=F32) for h in range(HEADS_PER_GROUP)]

    def absorb(j, tile_scores, diagonal):
        vt = vt_ref[:, pl.ds(pl.multiple_of(j * t, t), t)]
        for h, s in enumerate(tile_scores):
            lhs = jnp.concatenate([vt[h * HEAD_DIM:(h + 1) * HEAD_DIM, :], ones], axis=0)
            if diagonal:
                s = jnp.where(not_after, s, NEG_BIG)
                m_new = jnp.max(s, axis=0, keepdims=True)
                p = jnp.exp2(s - m_new).astype(BF16)
                acc_ref[h] = jnp.dot(lhs, p, preferred_element_type=F32)
            else:
                m_prev = m_ref[h]
                m_new = jnp.maximum(m_prev, jnp.max(s, axis=0, keepdims=True))
                p = jnp.exp2(s - m_new).astype(BF16)
                acc_ref[h] = jnp.exp2(m_prev - m_new) * acc_ref[h] + jnp.dot(lhs, p, preferred_element_type=F32)
            m_ref[h] = m_new

    absorb(i, scores(i), True)

    qf = qt.astype(F32)
    slack = []
    for h in range(HEADS_PER_GROUP):
        qh = qf[h * HEAD_DIM:(h + 1) * HEAD_DIM, :]
        q_norm = jnp.sqrt(jnp.sum(qh * qh, axis=0, keepdims=True))
        slack.append(jnp.max(q_norm * kmax_ref[g * HEADS_PER_GROUP + h] - m_ref[h]))

    def cond(j):
        jj = jnp.maximum(j, 0)
        live = None
        for h in range(HEADS_PER_GROUP):
            alive = slack[h] + nfe_ref[jj * N_HEADS + g * HEADS_PER_GROUP + h] > -FOX_DEAD
            live = alive if live is None else live | alive
        return (j >= 0) & live

    def stage(j, dst_ref):
        for h, s in enumerate(scores(jnp.maximum(j, 0))):
            dst_ref[h] = s

    @pl.when(cond(i - 1))
    def _():
        stage(i - 1, cur_ref)

    def body(j):
        stage(j - 1, nxt_ref)
        absorb(j, [cur_ref[h] for h in range(HEADS_PER_GROUP)], False)
        cur_ref[...] = nxt_ref[...]
        return j - 1

    lax.while_loop(cond, body, i - 1)
    outs = []
    for h in range(HEADS_PER_GROUP):
        acc = acc_ref[h]
        outs.append(acc[0:HEAD_DIM, :] / acc[HEAD_DIM:HEAD_DIM + 1, :])
    o_ref[...] = jnp.concatenate(outs, axis=0).T.astype(o_ref.dtype)


def _fox_call(nfe, kmax, qt, k, vt):
    s = k.shape[0]
    t = FOX_T
    smem = pl.BlockSpec(memory_space=pltpu.SMEM)
    return pl.pallas_call(
        _fox_kernel,
        grid=(N_GROUPS, s // t),
        in_specs=[smem, smem,
                  pl.BlockSpec((LANES, t), lambda g, i: (g, i)),
                  pl.BlockSpec((s, 2 * LANES), lambda g, i: (0, g)),
                  pl.BlockSpec((LANES, s), lambda g, i: (g, 0))],
        out_specs=pl.BlockSpec((t, LANES), lambda g, i: (i, g)),
        out_shape=jax.ShapeDtypeStruct((s, D_BRANCH), BF16),
        scratch_shapes=[pltpu.VMEM((HEADS_PER_GROUP, HEAD_DIM + ONES_ROWS, t), F32),
                        pltpu.VMEM((HEADS_PER_GROUP, 1, t), F32),
                        pltpu.VMEM((HEADS_PER_GROUP, t, t), F32),
                        pltpu.VMEM((HEADS_PER_GROUP, t, t), F32)],
        compiler_params=pltpu.CompilerParams(dimension_semantics=("arbitrary", "arbitrary"),
                                             vmem_limit_bytes=VMEM_LIMIT),
        name="forgetting_attn",
    )(nfe, kmax, qt, k, vt)


def _post_kernel(x_ref, osb_ref, ofx_ref, sh1_ref, sc1_ref, gt1_ref, sh2_ref, sc2_ref, gt2_ref,
                 gmix_ref, gmlp_ref, gfin_ref, wg_ref, bg_ref, wsb_ref, wfx_ref, wo_ref,
                 w1_ref, b1_ref, w2_ref, b2_ref, o_ref):
    d = x_ref.shape[1]
    x = x_ref[...]
    h1 = _rmsnorm(x, gmix_ref[...]) * (1.0 + sc1_ref[0:1, :]) + sh1_ref[0:1, :]
    h1b = h1.astype(BF16)
    branches = (jnp.dot(osb_ref[...], wsb_ref[...], preferred_element_type=F32),
                jnp.dot(ofx_ref[...], wfx_ref[...], preferred_element_type=F32))
    merged = None
    for b, proj in enumerate(branches):
        glogit = jnp.dot(h1b, wg_ref[:, b * d:(b + 1) * d], preferred_element_type=F32) \
            + bg_ref[:, b * d:(b + 1) * d]
        term = proj / (1.0 + jnp.exp(-glogit))
        merged = term if merged is None else merged + term
    x1 = x + gt1_ref[0:1, :] * jnp.dot(merged.astype(BF16), wo_ref[...], preferred_element_type=F32)

    h2 = _rmsnorm(x1, gmlp_ref[...]) * (1.0 + sc2_ref[0:1, :]) + sh2_ref[0:1, :]
    h2b = h2.astype(BF16)
    ff = None
    for c in range(w1_ref.shape[1] // FF_CHUNK):
        lo, hi = c * FF_CHUNK, (c + 1) * FF_CHUNK
        u = jnp.dot(h2b, w1_ref[:, lo:hi], preferred_element_type=F32) + b1_ref[:, lo:hi]
        u = jnp.square(jnp.maximum(u, 0.0))
        part = jnp.dot(u.astype(BF16), w2_ref[lo:hi, :], preferred_element_type=F32)
        ff = part if ff is None else ff + part
    x2 = x1 + gt2_ref[0:1, :] * (ff + b2_ref[...])
    o_ref[...] = _rmsnorm(x2, gfin_ref[...])


def _post_call(x2, o_sb, o_fx, mod, g_mix, g_mlp, g_final, w_gate, b_gate, w_out_sb, w_out_fox,
               w_o, w_ff1, b_ff1, w_ff2, b_ff2):
    s, d = x2.shape
    ts = TS_POST
    tok = lambda i: (i, 0)
    const = lambda i: (0, 0)

    def resident(a):
        return pl.BlockSpec(a.shape, const, pipeline_mode=pl.Buffered(1))

    mod_specs = [pl.BlockSpec((8, d), functools.partial(lambda k, i: (0, k), k)) for k in range(N_MOD)]
    return pl.pallas_call(
        _post_kernel,
        grid=(s // ts,),
        in_specs=[pl.BlockSpec((ts, d), tok),
                  pl.BlockSpec((ts, D_BRANCH), tok),
                  pl.BlockSpec((ts, D_BRANCH), tok)]
                 + mod_specs
                 + [resident(a) for a in (g_mix, g_mlp, g_final, w_gate, b_gate, w_out_sb, w_out_fox,
                                          w_o, w_ff1, b_ff1, w_ff2, b_ff2)],
        out_specs=pl.BlockSpec((ts, d), tok),
        out_shape=jax.ShapeDtypeStruct((s, d), F32),
        compiler_params=pltpu.CompilerParams(dimension_semantics=("arbitrary",),
                                             vmem_limit_bytes=VMEM_LIMIT),
        name="merge_mlp_norm",
    )(x2, o_sb, o_fx, *([mod] * N_MOD), g_mix, g_mlp, g_final, w_gate, b_gate, w_out_sb, w_out_fox,
      w_o, w_ff1, b_ff1, w_ff2, b_ff2)


def kernel(x, c, w_ada, b_ada, g_mix, w_in, b_f, w_gate, b_gate, w_out_sb, w_out_fox, w_o, g_mlp,
           w_ff1, b_ff1, w_ff2, b_ff2, g_final):
    batch, s, d = x.shape
    n_qkv = 6 * D_BRANCH
    assert w_ada.shape[0] == 1, "the final norm is fused into the single layer's last kernel"
    assert batch == 1 and c.shape == (1, d) and w_in.shape[2] == n_qkv + N_HEADS
    assert s % TS_PROJ == 0 and s % TS_POST == 0 and s % SB_T == 0 and s % FOX_T == 0
    assert TS_PROJ == FOX_T, "the projection kernel emits one tile-end forget sum per token block"

    xs = x.reshape(s, d)
    c8 = jnp.broadcast_to(c, (8, d))
    row = lambda a: a.reshape(1, -1)
    cols = lambda j: w_in[0, :, j * D_BRANCH:(j + 1) * D_BRANCH]

    mod = _mod_call(c8, w_ada[0], row(b_ada[0]))
    w_nn = jnp.concatenate([cols(1), cols(4)], axis=1).astype(BF16)
    w_f = w_in[0, :, n_qkv:]
    w_f_hi = w_f.astype(BF16)
    w_f_lo = (w_f - w_f_hi.astype(F32)).astype(BF16)
    w_nt = jnp.concatenate([cols(0).astype(BF16), cols(2).astype(BF16), cols(3).astype(BF16),
                            cols(5).astype(BF16), w_f_hi, w_f_lo], axis=1).T
    b_f_lanes = jnp.broadcast_to(b_f[0][:, None], (N_HEADS, LANES))
    qt_sb, k_sb, vt_sb, qt_fx, k_fx, vt_fx, nf_end, k_max = _proj_call(
        xs, mod, row(g_mix[0]), w_nn, w_nt, b_f_lanes)
    o_sb = _sb_call(qt_sb, k_sb, vt_sb)
    nfe = nf_end[:, :, 0].reshape(-1)
    o_fx = _fox_call(nfe, k_max[0, :N_HEADS], qt_fx, k_fx, vt_fx)
    out = _post_call(xs, o_sb, o_fx, mod, row(g_mix[0]), row(g_mlp[0]), row(g_final),
                     w_gate[0].astype(BF16), row(b_gate[0]),
                     w_out_sb[0].astype(BF16), w_out_fox[0].astype(BF16),
                     w_o[0].astype(BF16), w_ff1[0].astype(BF16), row(b_ff1[0]),
                     w_ff2[0].astype(BF16), row(b_ff2[0]))
    return out.reshape(batch, s, d)
```

```python
import functools

import jax
import jax.numpy as jnp
from jax import lax
from jax.experimental import pallas as pl
from jax.experimental.pallas import tpu as pltpu

F32 = jnp.float32
BF16 = jnp.bfloat16

HEAD_DIM = 64
N_HEADS = 8
D_BRANCH = N_HEADS * HEAD_DIM
N_MOD = 6
EPS = 1e-6

LANES = 128
HEADS_PER_GROUP = LANES // HEAD_DIM
N_GROUPS = D_BRANCH // LANES

TS_PROJ = 512
TS_POST = 512
SB_T = 256
SB_Q = 512
FOX_T = 512
FF_CHUNK = 1024
BIAS_TERMS = 3
ONES_ROWS = 16

LOG2E = 1.4426950408889634
Q_SCALE = HEAD_DIM ** -0.5 * LOG2E
STICK_DEAD = 104.0 * LOG2E
FOX_DEAD = 105.0 * LOG2E
NEG_BIG = -1e30
VMEM_LIMIT = 56 * 1024 * 1024

_NT = (((1,), (1,)), ((), ()))


def _rmsnorm(x, g):
    return x * lax.rsqrt(jnp.mean(x * x, axis=-1, keepdims=True) + EPS) * g


def _softplus2(y):
    return jnp.maximum(y, 0.0) + jnp.log2(1.0 + jnp.exp2(-jnp.abs(y)))


def _split2(x):
    hi = x.astype(BF16)
    return hi, (x - hi.astype(F32)).astype(BF16)


def _split3(x):
    hi = x.astype(BF16)
    r = x - hi.astype(F32)
    mid = r.astype(BF16)
    return hi, mid, (r - mid.astype(F32)).astype(BF16)


def _head_rows(qt, h):
    sub = lax.broadcasted_iota(jnp.int32, qt.shape, 0)
    return jnp.where((sub >= h * HEAD_DIM) & (sub < (h + 1) * HEAD_DIM), qt, jnp.zeros_like(qt))


def _mod_kernel(c_ref, w_ref, b_ref, o_ref):
    row = jnp.sum(c_ref[...] * w_ref[...], axis=0, keepdims=True) + b_ref[...]
    o_ref[...] = jnp.broadcast_to(row, o_ref.shape)


def _mod_call(c_col, w_ada, b_ada):
    d, n = w_ada.shape
    return pl.pallas_call(
        _mod_kernel,
        grid=(n // d,),
        in_specs=[pl.BlockSpec((d, 1), lambda j: (0, 0)),
                  pl.BlockSpec((d, d), lambda j: (0, j)),
                  pl.BlockSpec((1, d), lambda j: (0, j))],
        out_specs=pl.BlockSpec((8, d), lambda j: (0, j)),
        out_shape=jax.ShapeDtypeStruct((8, n), F32),
        name="adaln_mod",
    )(c_col, w_ada, b_ada)


def _proj_kernel(x_ref, sh_ref, sc_ref, g_ref, w_ref, wt_ref, bf_ref,
                 qst_ref, ks_ref, vst_ref, qft_ref, kf_ref, vft_ref, nfe_ref, kmax_ref, carry_ref):
    ts = x_ref.shape[0]
    reps = ts // LANES

    @pl.when(pl.program_id(0) == 0)
    def _():
        carry_ref[...] = jnp.zeros_like(carry_ref)
        kmax_ref[...] = jnp.zeros_like(kmax_ref)

    h = _rmsnorm(x_ref[...], g_ref[...]) * (1.0 + sc_ref[0:1, :]) + sh_ref[0:1, :]
    hb = h.astype(BF16)

    def proj(j):
        return jnp.dot(hb, w_ref[:, j * D_BRANCH:(j + 1) * D_BRANCH], preferred_element_type=F32)

    def proj_t(lo, hi):
        return lax.dot_general(wt_ref[lo:hi, :], hb, _NT, preferred_element_type=F32)

    qst_ref[...] = (proj_t(0, D_BRANCH) * Q_SCALE).astype(BF16)
    vst_ref[...] = proj_t(D_BRANCH, 2 * D_BRANCH).astype(BF16)
    qft_ref[...] = (proj_t(2 * D_BRANCH, 3 * D_BRANCH) * Q_SCALE).astype(BF16)
    vft_ref[...] = proj_t(3 * D_BRANCH, 4 * D_BRANCH).astype(BF16)
    ks_ref[...] = proj(0).astype(BF16)
    k_fox = proj(1).astype(BF16)

    kf32 = k_fox.astype(F32)
    seg_r = lax.broadcasted_iota(jnp.int32, (D_BRANCH, LANES), 0)
    seg_c = lax.broadcasted_iota(jnp.int32, (D_BRANCH, LANES), 1)
    seg = jnp.where(seg_r // HEAD_DIM == seg_c, 1.0, 0.0).astype(BF16)
    sq_hi, sq_lo = _split2(kf32 * kf32)
    norm2 = (jnp.dot(sq_hi, seg, preferred_element_type=F32)
             + jnp.dot(sq_lo, seg, preferred_element_type=F32))
    block_max = jnp.max(jnp.sqrt(norm2), axis=0, keepdims=True)
    kmax_ref[...] = jnp.maximum(kmax_ref[...], jnp.broadcast_to(block_max, kmax_ref.shape))

    along = lambda a: jnp.concatenate([a] * reps, axis=1)
    rows = proj_t(4 * D_BRANCH, 4 * D_BRANCH + 2 * N_HEADS)
    logit = rows[0:N_HEADS] + rows[N_HEADS:2 * N_HEADS] + along(bf_ref[...])
    log_f = jnp.minimum(logit, 0.0) - jnp.log(1.0 + jnp.exp(-jnp.abs(logit)))

    r = lax.broadcasted_iota(jnp.int32, (ts, ts), 0)
    c = lax.broadcasted_iota(jnp.int32, (ts, ts), 1)
    upper = jnp.where(r <= c, 1.0, 0.0).astype(BF16)

    def stack(parts, n_rows):
        pad = jnp.zeros((n_rows - BIAS_TERMS * N_HEADS, ts), F32)
        return jnp.concatenate([p.astype(F32) for p in parts] + [pad], axis=0)

    run = jnp.dot(stack(_split3(log_f), 4 * N_HEADS).astype(BF16), upper, preferred_element_type=F32)
    cum = (run[0:N_HEADS] + run[N_HEADS:2 * N_HEADS] + run[2 * N_HEADS:3 * N_HEADS]
           + along(carry_ref[...]))
    carry_ref[...] = jnp.broadcast_to(cum[:, ts - 1:ts], carry_ref.shape)
    neg_cum = cum * (-LOG2E)

    nfe_ref[0] = jnp.broadcast_to(neg_cum[:, ts - 1:ts], (N_HEADS, LANES))

    terms = stack(_split3(neg_cum), LANES).T.astype(BF16)
    er = lax.broadcasted_iota(jnp.int32, (LANES, D_BRANCH), 0)
    ec = lax.broadcasted_iota(jnp.int32, (LANES, D_BRANCH), 1)
    head, term = er % N_HEADS, er // N_HEADS
    target = LANES * (head // HEADS_PER_GROUP) + BIAS_TERMS * (head % HEADS_PER_GROUP) + term
    place = jnp.where((ec == target) & (term < BIAS_TERMS), 1.0, 0.0).astype(BF16)
    bias = jnp.dot(terms, place, preferred_element_type=F32).astype(BF16)
    for g in range(N_GROUPS):
        kf_ref[:, 2 * g * LANES:(2 * g + 1) * LANES] = k_fox[:, g * LANES:(g + 1) * LANES]
        kf_ref[:, (2 * g + 1) * LANES:(2 * g + 2) * LANES] = bias[:, g * LANES:(g + 1) * LANES]


def _proj_call(x2, mod, g_mix, w_nn, w_nt, b_f):
    s, d = x2.shape
    ts = TS_PROJ
    tok = lambda i: (i, 0)
    tok_t = lambda i: (0, i)
    const = lambda i: (0, 0)
    resident = lambda a: pl.BlockSpec(a.shape, const, pipeline_mode=pl.Buffered(1))
    t_spec = pl.BlockSpec((D_BRANCH, ts), tok_t)
    t_shape = jax.ShapeDtypeStruct((D_BRANCH, s), BF16)
    return pl.pallas_call(
        _proj_kernel,
        grid=(s // ts,),
        in_specs=[pl.BlockSpec((ts, d), tok),
                  pl.BlockSpec((8, d), lambda i: (0, 0)),
                  pl.BlockSpec((8, d), lambda i: (0, 1)),
                  pl.BlockSpec((1, d), const),
                  resident(w_nn), resident(w_nt),
                  pl.BlockSpec((N_HEADS, LANES), const)],
        out_specs=[t_spec, pl.BlockSpec((ts, D_BRANCH), tok), t_spec,
                   t_spec, pl.BlockSpec((ts, 2 * D_BRANCH), tok), t_spec,
                   pl.BlockSpec((1, N_HEADS, LANES), lambda i: (i, 0, 0)),
                   pl.BlockSpec((8, LANES), const)],
        out_shape=[t_shape, jax.ShapeDtypeStruct((s, D_BRANCH), BF16), t_shape,
                   t_shape, jax.ShapeDtypeStruct((s, 2 * D_BRANCH), BF16), t_shape,
                   jax.ShapeDtypeStruct((s // ts, N_HEADS, LANES), F32),
                   jax.ShapeDtypeStruct((8, LANES), F32)],
        scratch_shapes=[pltpu.VMEM((N_HEADS, LANES), F32)],
        compiler_params=pltpu.CompilerParams(dimension_semantics=("arbitrary",),
                                             vmem_limit_bytes=VMEM_LIMIT),
        name="proj",
    )(x2, mod, mod, g_mix, w_nn, w_nt, b_f)


def _sb_kernel(qt_ref, k_ref, vt_ref, o_ref, acc_ref, carry_ref):
    t = SB_T
    half = t // 2
    n_sub = SB_Q // t
    i = pl.program_id(1)
    qt = qt_ref[...]
    rhs = [[_head_rows(qt[:, u * t:(u + 1) * t], h) for h in range(HEADS_PER_GROUP)] for u in range(n_sub)]
    key_pos = lax.broadcasted_iota(jnp.int32, (t, t), 0)
    query_pos = lax.broadcasted_iota(jnp.int32, (t, t), 1)
    before = key_pos < query_pos
    tri_r = lax.broadcasted_iota(jnp.int32, (half, t), 0)
    tri_c = lax.broadcasted_iota(jnp.int32, (half, t), 1)
    tri2 = jnp.where(tri_c % half >= tri_r, 1.0, 0.0).astype(BF16)

    def keys(j):
        return k_ref[pl.ds(pl.multiple_of(j * t, t), t), :]

    def values(j, h):
        return vt_ref[:, pl.ds(pl.multiple_of(j * t, t), t)][h * HEAD_DIM:(h + 1) * HEAD_DIM, :]

    def suffix_mass(y):
        hi, lo = _split2(_softplus2(y))
        early, late = [
            jnp.dot(tri2, jnp.concatenate([hi[c * half:(c + 1) * half], lo[c * half:(c + 1) * half]], axis=0),
                    preferred_element_type=F32)
            for c in range(2)]
        return jnp.concatenate([early + late[0:1, :], late], axis=0)

    def weighted_values(j, h, y, mass):
        w = jnp.exp2(y - mass).astype(BF16)
        return jnp.dot(values(j, h), w, preferred_element_type=F32)

    def first_tiles(subs):
        chains = []
        for u, has_prev in subs:
            diag = i * n_sub + u
            tiles = [(diag, True)] + ([(diag - 1, False)] if has_prev else [])
            for j, masked in tiles:
                k = keys(j)
                for h in range(HEADS_PER_GROUP):
                    y = jnp.dot(k, rhs[u][h], preferred_element_type=F32)
                    if masked:
                        y = jnp.where(before, y, NEG_BIG)
                    chains.append({"u": u, "h": h, "j": j, "y": y})
        for c in chains:
            c["mass"] = suffix_mass(c["y"])
        for c in chains:
            c["out"] = weighted_values(c["j"], c["h"], c["y"], c["mass"])
        for u, has_prev in subs:
            cols = slice(u * t, (u + 1) * t)
            for h in range(HEADS_PER_GROUP):
                mine = [c for c in chains if c["u"] == u and c["h"] == h]
                carry = mine[0]["mass"][0:1, :]
                out = mine[0]["out"]
                if has_prev:
                    out = out + jnp.exp2(-carry) * mine[1]["out"]
                    carry = carry + mine[1]["mass"][0:1, :]
                acc_ref[h, :, cols] = out
                carry_ref[h, :, cols] = carry

    @pl.when(i == 0)
    def _():
        first_tiles([(0, False)] + [(u, True) for u in range(1, n_sub)])

    @pl.when(i > 0)
    def _():
        first_tiles([(u, True) for u in range(n_sub)])

    for u in range(n_sub):
        cols = slice(u * t, (u + 1) * t)

        def cond(state):
            j, low = state
            return (j >= 0) & (low <= STICK_DEAD)

        def body(state, u=u, cols=cols):
            j, _ = state
            k = keys(j)
            for h in range(HEADS_PER_GROUP):
                y = jnp.dot(k, rhs[u][h], preferred_element_type=F32)
                mass = suffix_mass(y)
                carry = carry_ref[h, :, cols]
                acc_ref[h, :, cols] += jnp.exp2(-carry) * weighted_values(j, h, y, mass)
                carry_ref[h, :, cols] = carry + mass[0:1, :]
            return j - 1, jnp.min(carry_ref[:, :, cols])

        lax.while_loop(cond, body, (i * n_sub + u - 2, jnp.min(carry_ref[:, :, cols])))
    out = jnp.concatenate([acc_ref[h] for h in range(HEADS_PER_GROUP)], axis=0)
    o_ref[...] = out.T.astype(o_ref.dtype)


def _sb_call(qt, k, vt):
    s = k.shape[0]
    return pl.pallas_call(
        _sb_kernel,
        grid=(N_GROUPS, s // SB_Q),
        in_specs=[pl.BlockSpec((LANES, SB_Q), lambda g, i: (g, i)),
                  pl.BlockSpec((s, LANES), lambda g, i: (0, g)),
                  pl.BlockSpec((LANES, s), lambda g, i: (g, 0))],
        out_specs=pl.BlockSpec((SB_Q, LANES), lambda g, i: (i, g)),
        out_shape=jax.ShapeDtypeStruct((s, D_BRANCH), BF16),
        scratch_shapes=[pltpu.VMEM((HEADS_PER_GROUP, HEAD_DIM, SB_Q), F32),
                        pltpu.VMEM((HEADS_PER_GROUP, 1, SB_Q), F32)],
        compiler_params=pltpu.CompilerParams(dimension_semantics=("arbitrary", "arbitrary"),
                                             vmem_limit_bytes=VMEM_LIMIT),
        name="stick_breaking_attn",
    )(qt, k, vt)


def _fox_kernel(nfe_ref, kmax_ref, qt_ref, k_ref, vt_ref, o_ref, acc_ref, m_ref, cur_ref, nxt_ref):
    t = FOX_T
    g = pl.program_id(0)
    i = pl.program_id(1)
    qt = qt_ref[...]
    sub = lax.broadcasted_iota(jnp.int32, (LANES, t), 0)
    rhs = []
    for h in range(HEADS_PER_GROUP):
        one_rows = jnp.where((sub >= h * BIAS_TERMS) & (sub < (h + 1) * BIAS_TERMS), 1.0, 0.0).astype(BF16)
        rhs.append(jnp.concatenate([_head_rows(qt, h), one_rows], axis=0))
    key_pos = lax.broadcasted_iota(jnp.int32, (t, t), 0)
    query_pos = lax.broadcasted_iota(jnp.int32, (t, t), 1)
    not_after = key_pos <= query_pos
    ones = jnp.ones((ONES_ROWS, t), BF16)

    def scores(j):
        k = k_ref[pl.ds(pl.multiple_of(j * t, t), t), :]
        return [jnp.dot(k, rhs[h], preferred_element_type=F32) for h in range(HEADS_PER_GROUP)]

    def absorb(j, tile_scores, diagonal):
        vt = vt_ref[:, pl.ds(pl.multiple_of(j * t, t), t)]
        for h, s in enumerate(tile_scores):
            lhs = jnp.concatenate([vt[h * HEAD_DIM:(h + 1) * HEAD_DIM, :], ones], axis=0)
            if diagonal:
                s = jnp.where(not_after, s, NEG_BIG)
                m_new = jnp.max(s, axis=0, keepdims=True)
                p = jnp.exp2(s - m_new).astype(BF16)
                acc_ref[h] = jnp.dot(lhs, p, preferred_element_type=F32)
            else:
                m_prev = m_ref[h]
                m_new = jnp.maximum(m_prev, jnp.max(s, axis=0, keepdims=True))
                p = jnp.exp2(s - m_new).astype(BF16)
                acc_ref[h] = jnp.exp2(m_prev - m_new) * acc_ref[h] + jnp.dot(lhs, p, preferred_element_type=F32)
            m_ref[h] = m_new

    def stage(j, dst_ref):
        for h, s in enumerate(scores(jnp.maximum(j, 0))):
            dst_ref[h] = s

    diag_scores = scores(i)
    stage(i - 1, cur_ref)
    absorb(i, diag_scores, True)

    qf = qt.astype(F32)
    slack = []
    for h in range(HEADS_PER_GROUP):
        qh = qf[h * HEAD_DIM:(h + 1) * HEAD_DIM, :]
        q_norm = jnp.sqrt(jnp.sum(qh * qh, axis=0, keepdims=True))
        slack.append(jnp.max(q_norm * kmax_ref[g * HEADS_PER_GROUP + h] - m_ref[h]))

    def cond(j):
        jj = jnp.maximum(j, 0)
        live = None
        for h in range(HEADS_PER_GROUP):
            alive = slack[h] + nfe_ref[jj * N_HEADS + g * HEADS_PER_GROUP + h] > -FOX_DEAD
            live = alive if live is None else live | alive
        return (j >= 0) & live

    def body(j):
        stage(j - 1, nxt_ref)
        absorb(j, [cur_ref[h] for h in range(HEADS_PER_GROUP)], False)
        cur_ref[...] = nxt_ref[...]
        return j - 1

    lax.while_loop(cond, body, i - 1)
    outs = []
    for h in range(HEADS_PER_GROUP):
        acc = acc_ref[h]
        outs.append(acc[0:HEAD_DIM, :] / acc[HEAD_DIM:HEAD_DIM + 1, :])
    o_ref[...] = jnp.concatenate(outs, axis=0).T.astype(o_ref.dtype)


def _fox_call(nfe, kmax, qt, k, vt):
    s = k.shape[0]
    t = FOX_T
    smem = pl.BlockSpec(memory_space=pltpu.SMEM)
    return pl.pallas_call(
        _fox_kernel,
        grid=(N_GROUPS, s // t),
        in_specs=[smem, smem,
                  pl.BlockSpec((LANES, t), lambda g, i: (g, i)),
                  pl.BlockSpec((s, 2 * LANES), lambda g, i: (0, g)),
                  pl.BlockSpec((LANES, s), lambda g, i: (g, 0))],
        out_specs=pl.BlockSpec((t, LANES), lambda g, i: (i, g)),
        out_shape=jax.ShapeDtypeStruct((s, D_BRANCH), BF16),
        scratch_shapes=[pltpu.VMEM((HEADS_PER_GROUP, HEAD_DIM + ONES_ROWS, t), F32),
                        pltpu.VMEM((HEADS_PER_GROUP, 1, t), F32),
                        pltpu.VMEM((HEADS_PER_GROUP, t, t), F32),
                        pltpu.VMEM((HEADS_PER_GROUP, t, t), F32)],
        compiler_params=pltpu.CompilerParams(dimension_semantics=("arbitrary", "arbitrary"),
                                             vmem_limit_bytes=VMEM_LIMIT),
        name="forgetting_attn",
    )(nfe, kmax, qt, k, vt)


def _post_kernel(x_ref, osb_ref, ofx_ref, sh1_ref, sc1_ref, gt1_ref, sh2_ref, sc2_ref, gt2_ref,
                 gmix_ref, gmlp_ref, gfin_ref, wg_ref, bg_ref, wsb_ref, wfx_ref, wo_ref,
                 w1_ref, b1_ref, w2_ref, b2_ref, o_ref):
    d = x_ref.shape[1]
    x = x_ref[...]
    h1 = _rmsnorm(x, gmix_ref[...]) * (1.0 + sc1_ref[0:1, :]) + sh1_ref[0:1, :]
    h1b = h1.astype(BF16)
    branches = (jnp.dot(osb_ref[...], wsb_ref[...], preferred_element_type=F32),
                jnp.dot(ofx_ref[...], wfx_ref[...], preferred_element_type=F32))
    merged = None
    for b, proj in enumerate(branches):
        glogit = jnp.dot(h1b, wg_ref[:, b * d:(b + 1) * d], preferred_element_type=F32) \
            + bg_ref[:, b * d:(b + 1) * d]
        term = proj / (1.0 + jnp.exp(-glogit))
        merged = term if merged is None else merged + term
    x1 = x + gt1_ref[0:1, :] * jnp.dot(merged.astype(BF16), wo_ref[...], preferred_element_type=F32)

    h2 = _rmsnorm(x1, gmlp_ref[...]) * (1.0 + sc2_ref[0:1, :]) + sh2_ref[0:1, :]
    h2b = h2.astype(BF16)
    ff = None
    for c in range(w1_ref.shape[1] // FF_CHUNK):
        lo, hi = c * FF_CHUNK, (c + 1) * FF_CHUNK
        u = jnp.dot(h2b, w1_ref[:, lo:hi], preferred_element_type=F32) + b1_ref[:, lo:hi]
        u = jnp.square(jnp.maximum(u, 0.0))
        part = jnp.dot(u.astype(BF16), w2_ref[lo:hi, :], preferred_element_type=F32)
        ff = part if ff is None else ff + part
    x2 = x1 + gt2_ref[0:1, :] * (ff + b2_ref[...])
    o_ref[...] = _rmsnorm(x2, gfin_ref[...])


def _post_call(x2, o_sb, o_fx, mod, g_mix, g_mlp, g_final, w_gate, b_gate, w_out_sb, w_out_fox,
               w_o, w_ff1, b_ff1, w_ff2, b_ff2):
    s, d = x2.shape
    ts = TS_POST
    tok = lambda i: (i, 0)
    const = lambda i: (0, 0)

    def resident(a):
        return pl.BlockSpec(a.shape, const, pipeline_mode=pl.Buffered(1))

    mod_specs = [pl.BlockSpec((8, d), functools.partial(lambda k, i: (0, k), k)) for k in range(N_MOD)]
    return pl.pallas_call(
        _post_kernel,
        grid=(s // ts,),
        in_specs=[pl.BlockSpec((ts, d), tok),
                  pl.BlockSpec((ts, D_BRANCH), tok),
                  pl.BlockSpec((ts, D_BRANCH), tok)]
                 + mod_specs
                 + [resident(a) for a in (g_mix, g_mlp, g_final, w_gate, b_gate, w_out_sb, w_out_fox,
                                          w_o, w_ff1, b_ff1, w_ff2, b_ff2)],
        out_specs=pl.BlockSpec((ts, d), tok),
        out_shape=jax.ShapeDtypeStruct((s, d), F32),
        compiler_params=pltpu.CompilerParams(dimension_semantics=("arbitrary",),
                                             vmem_limit_bytes=VMEM_LIMIT),
        name="merge_mlp_norm",
    )(x2, o_sb, o_fx, *([mod] * N_MOD), g_mix, g_mlp, g_final, w_gate, b_gate, w_out_sb, w_out_fox,
      w_o, w_ff1, b_ff1, w_ff2, b_ff2)


def kernel(x, c, w_ada, b_ada, g_mix, w_in, b_f, w_gate, b_gate, w_out_sb, w_out_fox, w_o, g_mlp,
           w_ff1, b_ff1, w_ff2, b_ff2, g_final):
    batch, s, d = x.shape
    n_qkv = 6 * D_BRANCH
    assert w_ada.shape[0] == 1, "the final norm is fused into the single layer's last kernel"
    assert batch == 1 and c.shape == (1, d) and w_in.shape[2] == n_qkv + N_HEADS
    assert s % TS_PROJ == 0 and s % TS_POST == 0 and s % SB_Q == 0 and SB_Q % SB_T == 0 and s % FOX_T == 0
    assert TS_PROJ == FOX_T, "the projection kernel emits one tile-end forget sum per token block"

    xs = x.reshape(s, d)
    row = lambda a: a.reshape(1, -1)
    cols = lambda j: w_in[0, :, j * D_BRANCH:(j + 1) * D_BRANCH]

    mod = _mod_call(c.reshape(d, 1), w_ada[0], row(b_ada[0]))
    w_nn = jnp.concatenate([cols(1), cols(4)], axis=1).astype(BF16)
    w_f = w_in[0, :, n_qkv:]
    w_f_hi = w_f.astype(BF16)
    w_f_lo = (w_f - w_f_hi.astype(F32)).astype(BF16)
    w_nt = jnp.concatenate([cols(0).astype(BF16), cols(2).astype(BF16), cols(3).astype(BF16),
                            cols(5).astype(BF16), w_f_hi, w_f_lo], axis=1).T
    b_f_lanes = jnp.broadcast_to(b_f[0][:, None], (N_HEADS, LANES))
    qt_sb, k_sb, vt_sb, qt_fx, k_fx, vt_fx, nf_end, k_max = _proj_call(
        xs, mod, row(g_mix[0]), w_nn, w_nt, b_f_lanes)
    o_sb = _sb_call(qt_sb, k_sb, vt_sb)
    nfe = nf_end[:, :, 0].reshape(-1)
    o_fx = _fox_call(nfe, k_max[0, :N_HEADS], qt_fx, k_fx, vt_fx)
    out = _post_call(xs, o_sb, o_fx, mod, row(g_mix[0]), row(g_mlp[0]), row(g_final),
                     w_gate[0].astype(BF16), row(b_gate[0]),
                     w_out_sb[0].astype(BF16), w_out_fox[0].astype(BF16),
                     w_o[0].astype(BF16), w_ff1[0].astype(BF16), row(b_ff1[0]),
                     w_ff2[0].astype(BF16), row(b_ff2[0]))
    return out.reshape(batch, s, d)
```

```python
import functools

import jax
import jax.numpy as jnp
from jax import lax
from jax.experimental import pallas as pl
from jax.experimental.pallas import tpu as pltpu

F32 = jnp.float32
BF16 = jnp.bfloat16

HEAD_DIM = 64
N_HEADS = 8
D_BRANCH = N_HEADS * HEAD_DIM
N_MOD = 6
EPS = 1e-6

LANES = 128
BF16_SUBLANES = 16
HEADS_PER_GROUP = LANES // HEAD_DIM
N_GROUPS = D_BRANCH // LANES

TS_PROJ = 512
TS_POST = 512
SB_T = 256
SB_Q = 1024
FOX_T = 512
FF_CHUNK = 1024
BIAS_TERMS = 3
ONES_ROWS = 16

LOG2E = 1.4426950408889634
Q_SCALE = HEAD_DIM ** -0.5 * LOG2E
STICK_DEAD = 104.0 * LOG2E
FOX_DEAD = 105.0 * LOG2E
NEG_BIG = -1e30
EXP2_CAP = 126.0
VMEM_LIMIT = 56 * 1024 * 1024

_NT = (((1,), (1,)), ((), ()))


def _rmsnorm(x, g):
    return x * lax.rsqrt(jnp.mean(x * x, axis=-1, keepdims=True) + EPS) * g


def _softplus2(y):
    return jnp.maximum(y, jnp.log2(1.0 + jnp.exp2(jnp.minimum(y, EXP2_CAP))))


def _split2(x):
    hi = x.astype(BF16)
    return hi, (x - hi.astype(F32)).astype(BF16)


def _split3(x):
    hi = x.astype(BF16)
    r = x - hi.astype(F32)
    mid = r.astype(BF16)
    return hi, mid, (r - mid.astype(F32)).astype(BF16)


def _head_rows(qt, h):
    sub = lax.broadcasted_iota(jnp.int32, qt.shape, 0)
    return jnp.where((sub >= h * HEAD_DIM) & (sub < (h + 1) * HEAD_DIM), qt, jnp.zeros_like(qt))


def _mod_kernel(c_ref, w_ref, b_ref, o_ref):
    row = jnp.sum(c_ref[...] * w_ref[...], axis=0, keepdims=True) + b_ref[...]
    o_ref[...] = jnp.broadcast_to(row, o_ref.shape)


def _mod_call(c_col, w_ada, b_ada):
    d, n = w_ada.shape
    return pl.pallas_call(
        _mod_kernel,
        grid=(n // d,),
        in_specs=[pl.BlockSpec((d, 1), lambda j: (0, 0)),
                  pl.BlockSpec((d, d), lambda j: (0, j)),
                  pl.BlockSpec((1, d), lambda j: (0, j))],
        out_specs=pl.BlockSpec((8, d), lambda j: (0, j)),
        out_shape=jax.ShapeDtypeStruct((8, n), F32),
        name="adaln_mod",
    )(c_col, w_ada, b_ada)


def _proj_kernel(n_cast, x_ref, sh_ref, sc_ref, g_ref, w_ref, wt_ref, bf_ref, *refs):
    f32_weight_refs = refs[:n_cast]
    qst_ref, ks_ref, vst_ref, qft_ref, kf_ref, vft_ref, nfe_ref, kmax_ref = refs[n_cast:n_cast + 8]
    bf16_weight_refs = refs[n_cast + 8:2 * n_cast + 8]
    carry_ref, = refs[2 * n_cast + 8:]
    ts = x_ref.shape[0]
    reps = ts // LANES

    for src, dst in zip(f32_weight_refs, bf16_weight_refs):
        dst[...] = src[...].astype(BF16)

    @pl.when(pl.program_id(0) == 0)
    def _():
        carry_ref[...] = jnp.zeros_like(carry_ref)
        kmax_ref[...] = jnp.zeros_like(kmax_ref)

    h = _rmsnorm(x_ref[...], g_ref[...]) * (1.0 + sc_ref[0:1, :]) + sh_ref[0:1, :]
    hb = h.astype(BF16)

    def proj(j):
        return jnp.dot(hb, w_ref[:, j * D_BRANCH:(j + 1) * D_BRANCH], preferred_element_type=F32)

    def proj_t(lo, hi):
        return lax.dot_general(wt_ref[lo:hi, :], hb, _NT, preferred_element_type=F32)

    qst_ref[...] = (proj_t(0, D_BRANCH) * Q_SCALE).astype(BF16)
    vst_ref[...] = proj_t(D_BRANCH, 2 * D_BRANCH).astype(BF16)
    qft_ref[...] = (proj_t(2 * D_BRANCH, 3 * D_BRANCH) * Q_SCALE).astype(BF16)
    vft_ref[...] = proj_t(3 * D_BRANCH, 4 * D_BRANCH).astype(BF16)
    ks_ref[...] = proj(0).astype(BF16)
    k_fox = proj(1).astype(BF16)

    kf32 = k_fox.astype(F32)
    seg_r = lax.broadcasted_iota(jnp.int32, (D_BRANCH, LANES), 0)
    seg_c = lax.broadcasted_iota(jnp.int32, (D_BRANCH, LANES), 1)
    seg = jnp.where(seg_r // HEAD_DIM == seg_c, 1.0, 0.0).astype(BF16)
    sq_hi, sq_lo = _split2(kf32 * kf32)
    norm2 = (jnp.dot(sq_hi, seg, preferred_element_type=F32)
             + jnp.dot(sq_lo, seg, preferred_element_type=F32))
    block_max = jnp.max(jnp.sqrt(norm2), axis=0, keepdims=True)
    kmax_ref[...] = jnp.maximum(kmax_ref[...], jnp.broadcast_to(block_max, kmax_ref.shape))

    along = lambda a: jnp.concatenate([a] * reps, axis=1)
    rows = proj_t(4 * D_BRANCH, 4 * D_BRANCH + 2 * N_HEADS)
    logit = rows[0:N_HEADS] + rows[N_HEADS:2 * N_HEADS] + along(bf_ref[...])
    log_f = jnp.minimum(logit, 0.0) - jnp.log(1.0 + jnp.exp(-jnp.abs(logit)))

    r = lax.broadcasted_iota(jnp.int32, (ts, ts), 0)
    c = lax.broadcasted_iota(jnp.int32, (ts, ts), 1)
    upper = jnp.where(r <= c, 1.0, 0.0).astype(BF16)

    def stack(parts, n_rows):
        pad = jnp.zeros((n_rows - BIAS_TERMS * N_HEADS, ts), F32)
        return jnp.concatenate([p.astype(F32) for p in parts] + [pad], axis=0)

    run = jnp.dot(stack(_split3(log_f), 4 * N_HEADS).astype(BF16), upper, preferred_element_type=F32)
    cum = (run[0:N_HEADS] + run[N_HEADS:2 * N_HEADS] + run[2 * N_HEADS:3 * N_HEADS]
           + along(carry_ref[...]))
    carry_ref[...] = jnp.broadcast_to(cum[:, ts - 1:ts], carry_ref.shape)
    neg_cum = cum * (-LOG2E)

    nfe_ref[0] = jnp.broadcast_to(neg_cum[:, ts - 1:ts], (N_HEADS, LANES))

    terms = stack(_split3(neg_cum), LANES).T.astype(BF16)
    er = lax.broadcasted_iota(jnp.int32, (LANES, D_BRANCH), 0)
    ec = lax.broadcasted_iota(jnp.int32, (LANES, D_BRANCH), 1)
    head, term = er % N_HEADS, er // N_HEADS
    target = LANES * (head // HEADS_PER_GROUP) + BIAS_TERMS * (head % HEADS_PER_GROUP) + term
    place = jnp.where((ec == target) & (term < BIAS_TERMS), 1.0, 0.0).astype(BF16)
    bias = jnp.dot(terms, place, preferred_element_type=F32).astype(BF16)
    for g in range(N_GROUPS):
        kf_ref[:, 2 * g * LANES:(2 * g + 1) * LANES] = k_fox[:, g * LANES:(g + 1) * LANES]
        kf_ref[:, (2 * g + 1) * LANES:(2 * g + 2) * LANES] = bias[:, g * LANES:(g + 1) * LANES]


def _proj_call(x2, mod, g_mix, w_nn, w_nt, b_f, f32_weights):
    s, d = x2.shape
    ts = TS_PROJ
    steps = s // ts
    tok = lambda i: (i, 0)
    tok_t = lambda i: (0, i)
    const = lambda i: (0, 0)
    resident = lambda a: pl.BlockSpec(a.shape, const, pipeline_mode=pl.Buffered(1))
    t_spec = pl.BlockSpec((D_BRANCH, ts), tok_t)
    t_shape = jax.ShapeDtypeStruct((D_BRANCH, s), BF16)
    for w in f32_weights:
        assert w.shape[0] % (steps * BF16_SUBLANES) == 0, w.shape
    slab_specs = [pl.BlockSpec((w.shape[0] // steps, w.shape[1]), tok) for w in f32_weights]
    return pl.pallas_call(
        functools.partial(_proj_kernel, len(f32_weights)),
        grid=(steps,),
        in_specs=[pl.BlockSpec((ts, d), tok),
                  pl.BlockSpec((8, d), lambda i: (0, 0)),
                  pl.BlockSpec((8, d), lambda i: (0, 1)),
                  pl.BlockSpec((1, d), const),
                  resident(w_nn), resident(w_nt),
                  pl.BlockSpec((N_HEADS, LANES), const)] + slab_specs,
        out_specs=[t_spec, pl.BlockSpec((ts, D_BRANCH), tok), t_spec,
                   t_spec, pl.BlockSpec((ts, 2 * D_BRANCH), tok), t_spec,
                   pl.BlockSpec((1, N_HEADS, LANES), lambda i: (i, 0, 0)),
                   pl.BlockSpec((8, LANES), const)] + slab_specs,
        out_shape=[t_shape, jax.ShapeDtypeStruct((s, D_BRANCH), BF16), t_shape,
                   t_shape, jax.ShapeDtypeStruct((s, 2 * D_BRANCH), BF16), t_shape,
                   jax.ShapeDtypeStruct((steps, N_HEADS, LANES), F32),
                   jax.ShapeDtypeStruct((8, LANES), F32)]
                  + [jax.ShapeDtypeStruct(w.shape, BF16) for w in f32_weights],
        scratch_shapes=[pltpu.VMEM((N_HEADS, LANES), F32)],
        compiler_params=pltpu.CompilerParams(dimension_semantics=("arbitrary",),
                                             vmem_limit_bytes=VMEM_LIMIT),
        name="proj",
    )(x2, mod, mod, g_mix, w_nn, w_nt, b_f, *f32_weights)


def _sb_kernel(qt_ref, k_ref, vt_ref, o_ref, acc_ref, carry_ref):
    t = SB_T
    half = t // 2
    n_sub = SB_Q // t
    i = pl.program_id(1)
    qt = qt_ref[...]
    rhs = [[_head_rows(qt[:, u * t:(u + 1) * t], h) for h in range(HEADS_PER_GROUP)] for u in range(n_sub)]
    key_pos = lax.broadcasted_iota(jnp.int32, (t, t), 0)
    query_pos = lax.broadcasted_iota(jnp.int32, (t, t), 1)
    before = key_pos < query_pos
    tri_r = lax.broadcasted_iota(jnp.int32, (half, t), 0)
    tri_c = lax.broadcasted_iota(jnp.int32, (half, t), 1)
    tri2 = jnp.where(tri_c % half >= tri_r, 1.0, 0.0).astype(BF16)

    def keys(j):
        return k_ref[pl.ds(pl.multiple_of(j * t, t), t), :]

    def values(j, h):
        return vt_ref[:, pl.ds(pl.multiple_of(j * t, t), t)][h * HEAD_DIM:(h + 1) * HEAD_DIM, :]

    def suffix_mass(y):
        hi, lo = _split2(_softplus2(y))
        early, late = [
            jnp.dot(tri2, jnp.concatenate([hi[c * half:(c + 1) * half], lo[c * half:(c + 1) * half]], axis=0),
                    preferred_element_type=F32)
            for c in range(2)]
        return jnp.concatenate([early + late[0:1, :], late], axis=0)

    def weighted_values(j, h, y, mass):
        w = jnp.exp2(y - mass).astype(BF16)
        return jnp.dot(values(j, h), w, preferred_element_type=F32)

    def first_tiles(subs):
        chains = []
        for u, has_prev in subs:
            diag = i * n_sub + u
            tiles = [(diag, True)] + ([(diag - 1, False)] if has_prev else [])
            for j, masked in tiles:
                k = keys(j)
                for h in range(HEADS_PER_GROUP):
                    y = jnp.dot(k, rhs[u][h], preferred_element_type=F32)
                    if masked:
                        y = jnp.where(before, y, NEG_BIG)
                    chains.append({"u": u, "h": h, "j": j, "y": y})
        for c in chains:
            c["mass"] = suffix_mass(c["y"])
        for c in chains:
            c["out"] = weighted_values(c["j"], c["h"], c["y"], c["mass"])
        for u, has_prev in subs:
            cols = slice(u * t, (u + 1) * t)
            for h in range(HEADS_PER_GROUP):
                mine = [c for c in chains if c["u"] == u and c["h"] == h]
                carry = mine[0]["mass"][0:1, :]
                out = mine[0]["out"]
                if has_prev:
                    out = out + jnp.exp2(-carry) * mine[1]["out"]
                    carry = carry + mine[1]["mass"][0:1, :]
                acc_ref[h, :, cols] = out
                carry_ref[h, :, cols] = carry

    @pl.when(i == 0)
    def _():
        first_tiles([(0, False)] + [(u, True) for u in range(1, n_sub)])

    @pl.when(i > 0)
    def _():
        first_tiles([(u, True) for u in range(n_sub)])

    for u in range(n_sub):
        cols = slice(u * t, (u + 1) * t)

        def cond(state):
            j, low = state
            return (j >= 0) & (low <= STICK_DEAD)

        def body(state, u=u, cols=cols):
            j, _ = state
            k = keys(j)
            for h in range(HEADS_PER_GROUP):
                y = jnp.dot(k, rhs[u][h], preferred_element_type=F32)
                mass = suffix_mass(y)
                carry = carry_ref[h, :, cols]
                acc_ref[h, :, cols] += jnp.exp2(-carry) * weighted_values(j, h, y, mass)
                carry_ref[h, :, cols] = carry + mass[0:1, :]
            return j - 1, jnp.min(carry_ref[:, :, cols])

        lax.while_loop(cond, body, (i * n_sub + u - 2, jnp.min(carry_ref[:, :, cols])))
    out = jnp.concatenate([acc_ref[h] for h in range(HEADS_PER_GROUP)], axis=0)
    o_ref[...] = out.T.astype(o_ref.dtype)


def _sb_call(qt, k, vt):
    s = k.shape[0]
    return pl.pallas_call(
        _sb_kernel,
        grid=(N_GROUPS, s // SB_Q),
        in_specs=[pl.BlockSpec((LANES, SB_Q), lambda g, i: (g, i)),
                  pl.BlockSpec((s, LANES), lambda g, i: (0, g)),
                  pl.BlockSpec((LANES, s), lambda g, i: (g, 0))],
        out_specs=pl.BlockSpec((SB_Q, LANES), lambda g, i: (i, g)),
        out_shape=jax.ShapeDtypeStruct((s, D_BRANCH), BF16),
        scratch_shapes=[pltpu.VMEM((HEADS_PER_GROUP, HEAD_DIM, SB_Q), F32),
                        pltpu.VMEM((HEADS_PER_GROUP, 1, SB_Q), F32)],
        compiler_params=pltpu.CompilerParams(dimension_semantics=("arbitrary", "arbitrary"),
                                             vmem_limit_bytes=VMEM_LIMIT),
        name="stick_breaking_attn",
    )(qt, k, vt)


def _fox_kernel(nfe_ref, kmax_ref, qt_ref, k_ref, vt_ref, o_ref, acc_ref, m_ref,
                sa_ref, ma_ref, sb_ref, mb_ref):
    t = FOX_T
    g = pl.program_id(0)
    i = pl.program_id(1)
    qt = qt_ref[...]
    sub = lax.broadcasted_iota(jnp.int32, (LANES, t), 0)
    rhs = []
    for h in range(HEADS_PER_GROUP):
        one_rows = jnp.where((sub >= h * BIAS_TERMS) & (sub < (h + 1) * BIAS_TERMS), 1.0, 0.0).astype(BF16)
        rhs.append(jnp.concatenate([_head_rows(qt, h), one_rows], axis=0))
    key_pos = lax.broadcasted_iota(jnp.int32, (t, t), 0)
    query_pos = lax.broadcasted_iota(jnp.int32, (t, t), 1)
    not_after = key_pos <= query_pos
    ones = jnp.ones((ONES_ROWS, t), BF16)

    def scores(j):
        k = k_ref[pl.ds(pl.multiple_of(j * t, t), t), :]
        return [jnp.dot(k, rhs[h], preferred_element_type=F32) for h in range(HEADS_PER_GROUP)]

    def values_and_ones(j, h):
        vt = vt_ref[:, pl.ds(pl.multiple_of(j * t, t), t)]
        return jnp.concatenate([vt[h * HEAD_DIM:(h + 1) * HEAD_DIM, :], ones], axis=0)

    diag_scores = scores(i)
    for h, s in enumerate(scores(jnp.maximum(i - 1, 0))):
        sa_ref[h] = s
        ma_ref[h] = jnp.max(s, axis=0, keepdims=True)
    for h, s in enumerate(diag_scores):
        s = jnp.where(not_after, s, NEG_BIG)
        m_new = jnp.max(s, axis=0, keepdims=True)
        acc_ref[h] = jnp.dot(values_and_ones(i, h), jnp.exp2(s - m_new).astype(BF16),
                             preferred_element_type=F32)
        m_ref[h] = m_new

    qf = qt.astype(F32)
    slack = []
    for h in range(HEADS_PER_GROUP):
        qh = qf[h * HEAD_DIM:(h + 1) * HEAD_DIM, :]
        q_norm = jnp.sqrt(jnp.sum(qh * qh, axis=0, keepdims=True))
        slack.append(jnp.max(q_norm * kmax_ref[g * HEADS_PER_GROUP + h] - m_ref[h]))

    def cond(j):
        jj = jnp.maximum(j, 0)
        live = None
        for h in range(HEADS_PER_GROUP):
            alive = slack[h] + nfe_ref[jj * N_HEADS + g * HEADS_PER_GROUP + h] > -FOX_DEAD
            live = alive if live is None else live | alive
        return (j >= 0) & live

    def body(j):
        for h, s in enumerate(scores(jnp.maximum(j - 1, 0))):
            sb_ref[h] = s
            mb_ref[h] = jnp.max(s, axis=0, keepdims=True)
        for h in range(HEADS_PER_GROUP):
            m_prev = m_ref[h]
            m_new = jnp.maximum(m_prev, ma_ref[h])
            p = jnp.exp2(sa_ref[h] - m_new).astype(BF16)
            acc_ref[h] = (jnp.exp2(m_prev - m_new) * acc_ref[h]
                          + jnp.dot(values_and_ones(j, h), p, preferred_element_type=F32))
            m_ref[h] = m_new
        sa_ref[...] = sb_ref[...]
        ma_ref[...] = mb_ref[...]
        return j - 1

    lax.while_loop(cond, body, i - 1)

    outs = []
    for h in range(HEADS_PER_GROUP):
        acc = acc_ref[h]
        outs.append(acc[0:HEAD_DIM, :] / acc[HEAD_DIM:HEAD_DIM + 1, :])
    o_ref[...] = jnp.concatenate(outs, axis=0).T.astype(o_ref.dtype)


def _fox_call(nfe, kmax, qt, k, vt):
    s = k.shape[0]
    t = FOX_T
    smem = pl.BlockSpec(memory_space=pltpu.SMEM)
    return pl.pallas_call(
        _fox_kernel,
        grid=(N_GROUPS, s // t),
        in_specs=[smem, smem,
                  pl.BlockSpec((LANES, t), lambda g, i: (g, i)),
                  pl.BlockSpec((s, 2 * LANES), lambda g, i: (0, g)),
                  pl.BlockSpec((LANES, s), lambda g, i: (g, 0))],
        out_specs=pl.BlockSpec((t, LANES), lambda g, i: (i, g)),
        out_shape=jax.ShapeDtypeStruct((s, D_BRANCH), BF16),
        scratch_shapes=[pltpu.VMEM((HEADS_PER_GROUP, HEAD_DIM + ONES_ROWS, t), F32),
                        pltpu.VMEM((HEADS_PER_GROUP, 1, t), F32)]
                       + [pltpu.VMEM((HEADS_PER_GROUP, t, t), F32),
                          pltpu.VMEM((HEADS_PER_GROUP, 1, t), F32)] * 2,
        compiler_params=pltpu.CompilerParams(dimension_semantics=("arbitrary", "arbitrary"),
                                             vmem_limit_bytes=VMEM_LIMIT),
        name="forgetting_attn",
    )(nfe, kmax, qt, k, vt)


def _post_kernel(x_ref, osb_ref, ofx_ref, sh1_ref, sc1_ref, gt1_ref, sh2_ref, sc2_ref, gt2_ref,
                 gmix_ref, gmlp_ref, gfin_ref, wg_ref, bg_ref, wsb_ref, wfx_ref, wo_ref,
                 w1_ref, b1_ref, w2_ref, b2_ref, o_ref):
    d = x_ref.shape[1]
    x = x_ref[...]
    h1 = _rmsnorm(x, gmix_ref[...]) * (1.0 + sc1_ref[0:1, :]) + sh1_ref[0:1, :]
    h1b = h1.astype(BF16)
    branches = (jnp.dot(osb_ref[...], wsb_ref[...], preferred_element_type=F32),
                jnp.dot(ofx_ref[...], wfx_ref[...], preferred_element_type=F32))
    merged = None
    for b, proj in enumerate(branches):
        glogit = jnp.dot(h1b, wg_ref[:, b * d:(b + 1) * d], preferred_element_type=F32) \
            + bg_ref[:, b * d:(b + 1) * d]
        term = proj / (1.0 + jnp.exp(-glogit))
        merged = term if merged is None else merged + term
    x1 = x + gt1_ref[0:1, :] * jnp.dot(merged.astype(BF16), wo_ref[...], preferred_element_type=F32)

    h2 = _rmsnorm(x1, gmlp_ref[...]) * (1.0 + sc2_ref[0:1, :]) + sh2_ref[0:1, :]
    h2b = h2.astype(BF16)
    ff = None
    for c in range(w1_ref.shape[1] // FF_CHUNK):
        lo, hi = c * FF_CHUNK, (c + 1) * FF_CHUNK
        u = jnp.dot(h2b, w1_ref[:, lo:hi], preferred_element_type=F32) + b1_ref[:, lo:hi]
        u = jnp.square(jnp.maximum(u, 0.0))
        part = jnp.dot(u.astype(BF16), w2_ref[lo:hi, :], preferred_element_type=F32)
        ff = part if ff is None else ff + part
    x2 = x1 + gt2_ref[0:1, :] * (ff + b2_ref[...])
    o_ref[...] = _rmsnorm(x2, gfin_ref[...])


def _post_call(x2, o_sb, o_fx, mod, g_mix, g_mlp, g_final, w_gate, b_gate, w_out_sb, w_out_fox,
               w_o, w_ff1, b_ff1, w_ff2, b_ff2):
    s, d = x2.shape
    ts = TS_POST
    tok = lambda i: (i, 0)
    const = lambda i: (0, 0)

    def resident(a):
        return pl.BlockSpec(a.shape, const, pipeline_mode=pl.Buffered(1))

    mod_specs = [pl.BlockSpec((8, d), functools.partial(lambda k, i: (0, k), k)) for k in range(N_MOD)]
    return pl.pallas_call(
        _post_kernel,
        grid=(s // ts,),
        in_specs=[pl.BlockSpec((ts, d), tok),
                  pl.BlockSpec((ts, D_BRANCH), tok),
                  pl.BlockSpec((ts, D_BRANCH), tok)]
                 + mod_specs
                 + [resident(a) for a in (g_mix, g_mlp, g_final, w_gate, b_gate, w_out_sb, w_out_fox,
                                          w_o, w_ff1, b_ff1, w_ff2, b_ff2)],
        out_specs=pl.BlockSpec((ts, d), tok),
        out_shape=jax.ShapeDtypeStruct((s, d), F32),
        compiler_params=pltpu.CompilerParams(dimension_semantics=("arbitrary",),
                                             vmem_limit_bytes=VMEM_LIMIT),
        name="merge_mlp_norm",
    )(x2, o_sb, o_fx, *([mod] * N_MOD), g_mix, g_mlp, g_final, w_gate, b_gate, w_out_sb, w_out_fox,
      w_o, w_ff1, b_ff1, w_ff2, b_ff2)


def kernel(x, c, w_ada, b_ada, g_mix, w_in, b_f, w_gate, b_gate, w_out_sb, w_out_fox, w_o, g_mlp,
           w_ff1, b_ff1, w_ff2, b_ff2, g_final):
    batch, s, d = x.shape
    n_qkv = 6 * D_BRANCH
    assert w_ada.shape[0] == 1, "the final norm is fused into the single layer's last kernel"
    assert batch == 1 and c.shape == (1, d) and w_in.shape[2] == n_qkv + N_HEADS
    assert s % TS_PROJ == 0 and s % TS_POST == 0 and s % SB_Q == 0 and SB_Q % SB_T == 0 and s % FOX_T == 0
    assert TS_PROJ == FOX_T, "the projection kernel emits one tile-end forget sum per token block"

    xs = x.reshape(s, d)
    row = lambda a: a.reshape(1, -1)
    cols = lambda j: w_in[0, :, j * D_BRANCH:(j + 1) * D_BRANCH]

    mod = _mod_call(c.reshape(d, 1), w_ada[0], row(b_ada[0]))
    w_nn = jnp.concatenate([cols(1), cols(4)], axis=1).astype(BF16)
    w_f = w_in[0, :, n_qkv:]
    w_f_hi = w_f.astype(BF16)
    w_f_lo = (w_f - w_f_hi.astype(F32)).astype(BF16)
    w_nt = jnp.concatenate([cols(0).astype(BF16), cols(2).astype(BF16), cols(3).astype(BF16),
                            cols(5).astype(BF16), w_f_hi, w_f_lo], axis=1).T
    b_f_lanes = jnp.broadcast_to(b_f[0][:, None], (N_HEADS, LANES))
    later_weights = [w_gate[0], w_out_sb[0], w_out_fox[0], w_o[0], w_ff1[0], w_ff2[0]]
    (qt_sb, k_sb, vt_sb, qt_fx, k_fx, vt_fx, nf_end, k_max,
     w_gate_b, w_out_sb_b, w_out_fox_b, w_o_b, w_ff1_b, w_ff2_b) = _proj_call(
        xs, mod, row(g_mix[0]), w_nn, w_nt, b_f_lanes, later_weights)
    o_sb = _sb_call(qt_sb, k_sb, vt_sb)
    nfe = nf_end[:, :, 0].reshape(-1)
    o_fx = _fox_call(nfe, k_max[0, :N_HEADS], qt_fx, k_fx, vt_fx)
    out = _post_call(xs, o_sb, o_fx, mod, row(g_mix[0]), row(g_mlp[0]), row(g_final),
                     w_gate_b, row(b_gate[0]), w_out_sb_b, w_out_fox_b, w_o_b,
                     w_ff1_b, row(b_ff1[0]), w_ff2_b, row(b_ff2[0]))
    return out.reshape(batch, s, d)
```
